```python
import math
import jax, jax.numpy as jnp
from jax import lax
import numpy as np

D_MODEL = 4096
BATCH = 4
SEQ = 2048
DEPTH = 2

CHUNK = 64
Q_BLOCK = 128
ATTN_HEAD_DIM = 128
ATTN_WIDTH = D_MODEL // 2
ATTN_HEADS = ATTN_WIDTH // ATTN_HEAD_DIM
SSM_WIDTH = D_MODEL // 2
SSM_GROUP = 16
SSM_GROUPS = SSM_WIDTH // SSM_GROUP
SSM_STATE = 64
D_FF = 4 * D_MODEL
PLE_DIM = 256
RMS_EPS = 1e-6
DT_MIN = 0.001
DT_MAX = 0.1
IN_WIDTH = 3 * ATTN_WIDTH + SSM_WIDTH + 2 * D_MODEL
SPLITS = [ATTN_WIDTH, 2 * ATTN_WIDTH, 3 * ATTN_WIDTH, 3 * ATTN_WIDTH + SSM_WIDTH,
          3 * ATTN_WIDTH + SSM_WIDTH + D_MODEL]

kernel_name = "hybrid_stickbreak_s5_gated_trunk"


def rmsnorm(x, g):
    xf = x.astype(jnp.float32)
    y = xf * lax.rsqrt(jnp.mean(xf * xf, axis=-1, keepdims=True) + RMS_EPS)
    return (y * g.astype(jnp.float32)).astype(x.dtype)


def stick_breaking_attention(q, k, v):
    b, s, h, dh = q.shape
    nb = s // Q_BLOCK
    scale = dh ** -0.5
    kf = k.astype(jnp.float32)
    vf = v.astype(jnp.float32)
    qb = q.astype(jnp.float32).reshape(b, nb, Q_BLOCK, h, dh).transpose(1, 0, 2, 3, 4)
    key_pos = jnp.arange(s)

    def block(args):
        q_blk, blk = args
        q_pos = blk * Q_BLOCK + jnp.arange(Q_BLOCK)
        mask = key_pos[None, :] < q_pos[:, None]
        z = jnp.einsum('bqhd,bkhd->bhqk', q_blk, kf) * scale
        log_beta = jax.nn.log_sigmoid(z)
        log_1m = jnp.where(mask, jax.nn.log_sigmoid(-z), 0.0)
        tail = lax.cumsum(log_1m, axis=3, reverse=True) - log_1m
        w = jnp.where(mask, jnp.exp(log_beta + tail), 0.0)
        return jnp.einsum('bhqk,bkhd->bqhd', w, vf)

    out = lax.map(block, (qb, jnp.arange(nb)))
    return out.transpose(1, 0, 2, 3, 4).reshape(b, s, h, dh).astype(q.dtype)


def s5_ssm(u, lam_re, lam_im, log_dt, b_re, b_im, c_re, c_im, d_skip):
    bsz, s, w = u.shape
    f32 = jnp.float32
    uf = u.astype(f32).reshape(bsz, s, SSM_GROUPS, SSM_GROUP)
    lam = lax.complex(lam_re.astype(f32), lam_im.astype(f32))
    dt = jnp.exp(log_dt.astype(f32))[:, None]
    lam_bar = jnp.exp(lam * dt)
    bmat = lax.complex(b_re.astype(f32), b_im.astype(f32))
    b_bar = ((lam_bar - 1.0) / lam)[..., None] * bmat
    cmat = lax.complex(c_re.astype(f32), c_im.astype(f32))
    bu = jnp.einsum('bsgc,gpc->sbgp', uf.astype(jnp.complex64), b_bar)
    a = jnp.broadcast_to(lam_bar[None, None], (s, 1, SSM_GROUPS, SSM_STATE))

    def combine(e1, e2):
        a1, x1 = e1
        a2, x2 = e2
        return a1 * a2, a2 * x1 + x2

    _, states = lax.associative_scan(combine, (a, bu), axis=0)
    y = jnp.einsum('sbgp,gcp->bsgc', states, cmat).real
    y = y + d_skip.astype(f32).reshape(SSM_GROUPS, SSM_GROUP) * uf
    return y.reshape(bsz, s, w).astype(u.dtype)


def setup_inputs(seed: int = 0) -> dict:
    key = jax.random.key(seed)
    ks = jax.random.split(key, 24)
    nrm = jax.random.normal
    L = DEPTH
    G, P, C = SSM_GROUPS, SSM_STATE, SSM_GROUP
    return {
        "x": nrm(ks[0], (BATCH, SEQ, D_MODEL), jnp.float32),
        "p": nrm(ks[1], (L, BATCH, SEQ, PLE_DIM), jnp.float32),
        "g_mix": 1.0 + 0.02 * nrm(ks[2], (L, D_MODEL), jnp.float32),
        "w_in": nrm(ks[3], (L, D_MODEL, IN_WIDTH), jnp.float32) * D_MODEL ** -0.5,
        "w_br_attn": nrm(ks[4], (L, ATTN_WIDTH, D_MODEL), jnp.float32) * ATTN_WIDTH ** -0.5,
        "lam_re": -0.5 + 0.01 * nrm(ks[5], (L, G, P), jnp.float32),
        "lam_im": math.pi * jnp.arange(P, dtype=jnp.float32)[None, None, :] + 0.01 * nrm(ks[6], (L, G, P), jnp.float32),
        "log_dt": jax.random.uniform(ks[7], (L, G), jnp.float32, math.log(DT_MIN), math.log(DT_MAX)),
        "b_re": nrm(ks[8], (L, G, P, C), jnp.float32) * (2.0 * C) ** -0.5,
        "b_im": nrm(ks[9], (L, G, P, C), jnp.float32) * (2.0 * C) ** -0.5,
        "c_re": nrm(ks[10], (L, G, C, P), jnp.float32) * (2.0 * P) ** -0.5,
        "c_im": nrm(ks[11], (L, G, C, P), jnp.float32) * (2.0 * P) ** -0.5,
        "d_skip": nrm(ks[12], (L, SSM_WIDTH), jnp.float32),
        "w_glu": nrm(ks[13], (L, SSM_WIDTH, SSM_WIDTH), jnp.float32) * SSM_WIDTH ** -0.5,
        "w_br_ssm": nrm(ks[14], (L, SSM_WIDTH, D_MODEL), jnp.float32) * SSM_WIDTH ** -0.5,
        "w_o": nrm(ks[15], (L, D_MODEL, D_MODEL), jnp.float32) * D_MODEL ** -0.5,
        "g_mlp": 1.0 + 0.02 * nrm(ks[16], (L, D_MODEL), jnp.float32),
        "w_ff1": nrm(ks[17], (L, D_MODEL, D_FF), jnp.float32) * D_MODEL ** -0.5,
        "w_ff2": nrm(ks[18], (L, D_FF, D_MODEL), jnp.float32) * D_FF ** -0.5,
        "g_ple": 1.0 + 0.02 * nrm(ks[19], (L, D_MODEL), jnp.float32),
        "w_ple_gate": nrm(ks[20], (L, D_MODEL, D_MODEL), jnp.float32) * D_MODEL ** -0.5,
        "w_ple": nrm(ks[21], (L, PLE_DIM, D_MODEL), jnp.float32) * PLE_DIM ** -0.5,
        "g_final": 1.0 + 0.02 * nrm(ks[22], (D_MODEL,), jnp.float32),
    }


def reference(x, p, g_mix, w_in, w_br_attn, lam_re, lam_im, log_dt, b_re, b_im,
              c_re, c_im, d_skip, w_glu, w_br_ssm, w_o, g_mlp, w_ff1, w_ff2,
              g_ple, w_ple_gate, w_ple, g_final):
    h = x
    bsz, s, _ = x.shape
    for i in range(DEPTH):
        xn = rmsnorm(h, g_mix[i])
        proj = xn @ w_in[i]
        q, k, v, u, g_a, g_s = jnp.split(proj, SPLITS, axis=-1)
        q = q.reshape(bsz, s, ATTN_HEADS, ATTN_HEAD_DIM)
        k = k.reshape(bsz, s, ATTN_HEADS, ATTN_HEAD_DIM)
        v = v.reshape(bsz, s, ATTN_HEADS, ATTN_HEAD_DIM)
        attn = stick_breaking_attention(q, k, v).reshape(bsz, s, ATTN_WIDTH)
        attn_up = attn @ w_br_attn[i]
        y = jax.nn.gelu(s5_ssm(u, lam_re[i], lam_im[i], log_dt[i], b_re[i], b_im[i],
                               c_re[i], c_im[i], d_skip[i]))
        ssm = y * jax.nn.sigmoid(y @ w_glu[i])
        ssm_up = ssm @ w_br_ssm[i]
        merged = jax.nn.sigmoid(g_a) * attn_up + jax.nn.sigmoid(g_s) * ssm_up
        h = h + merged @ w_o[i]
        hn = rmsnorm(h, g_mlp[i])
        h = h + jnp.square(jax.nn.relu(hn @ w_ff1[i])) @ w_ff2[i]
        gate = jax.nn.sigmoid(rmsnorm(h, g_ple[i]) @ w_ple_gate[i])
        h = h + (p[i] @ w_ple[i]) * gate
    return rmsnorm(h, g_final)
```

```python
import functools
import math

import jax
import jax.numpy as jnp
from jax import lax
from jax.experimental import pallas as pl
from jax.experimental.pallas import tpu as pltpu

F32 = jnp.float32
BF16 = jnp.bfloat16

RMS_EPS = 1e-6
HEAD_DIM = 128
SSM_GROUP = 16
SSM_STATE = 64
SLAB = 256
SLAB_GROUPS = SLAB // SSM_GROUP
SLAB_STATES = SLAB_GROUPS * SSM_STATE
SUBLANES = 8
V7X_VMEM_LIMIT = 56 * 1024 * 1024


def _tile(dim, pref):
    if dim % pref == 0:
        return pref
    assert dim < pref, (dim, pref)
    return dim


def _params(*sem):
    return pltpu.CompilerParams(dimension_semantics=sem, vmem_limit_bytes=V7X_VMEM_LIMIT)


def _dot(a, b):
    return jnp.dot(a, b, preferred_element_type=F32)


def _rmsnorm_kernel(x_ref, g_ref, o_ref):
    x = x_ref[...]
    ms = jnp.mean(x * x, axis=-1, keepdims=True)
    o_ref[...] = (x * lax.rsqrt(ms + RMS_EPS) * g_ref[...]).astype(o_ref.dtype)


def _rmsnorm(x, g, out_dtype):
    t, d = x.shape
    tm = _tile(t, 256)
    return pl.pallas_call(
        _rmsnorm_kernel,
        grid=(t // tm,),
        in_specs=[pl.BlockSpec((tm, d), lambda i: (i, 0)),
                  pl.BlockSpec((1, d), lambda i: (0, 0))],
        out_specs=pl.BlockSpec((tm, d), lambda i: (i, 0)),
        out_shape=jax.ShapeDtypeStruct((t, d), out_dtype),
        compiler_params=_params("parallel"),
        name="rmsnorm",
    )(x, g.reshape(1, d))


def _mm_kernel(x_ref, w_ref, o_ref, *, act):
    acc = _dot(x_ref[...], w_ref[...])
    if act == "sigmoid":
        acc = jax.nn.sigmoid(acc)
    elif act == "relu2":
        acc = jnp.square(jnp.maximum(acc, 0.0))
    o_ref[...] = acc.astype(o_ref.dtype)


def _matmul(x, w, *, col0, ncols, act, name):
    t, k = x.shape
    tm, tn = _tile(t, 1024), _tile(ncols, 1024)
    assert col0 % tn == 0
    joff = col0 // tn
    return pl.pallas_call(
        functools.partial(_mm_kernel, act=act),
        grid=(t // tm, ncols // tn),
        in_specs=[pl.BlockSpec((tm, k), lambda i, j: (i, 0)),
                  pl.BlockSpec((k, tn), lambda i, j: (0, j + joff))],
        out_specs=pl.BlockSpec((tm, tn), lambda i, j: (i, j)),
        out_shape=jax.ShapeDtypeStruct((t, ncols), BF16),
        compiler_params=_params("parallel", "parallel"),
        name=name,
    )(x, w)


def _glu_kernel(y_ref, w_ref, yt_ref, o_ref):
    gate = jax.nn.sigmoid(_dot(y_ref[...], w_ref[...]))
    o_ref[...] = (yt_ref[...].astype(F32) * gate).astype(o_ref.dtype)


def _glu(y, w):
    t, k = y.shape
    n = w.shape[1]
    tm, tn = _tile(t, 1024), _tile(n, 1024)
    return pl.pallas_call(
        _glu_kernel,
        grid=(t // tm, n // tn),
        in_specs=[pl.BlockSpec((tm, k), lambda i, j: (i, 0)),
                  pl.BlockSpec((k, tn), lambda i, j: (0, j)),
                  pl.BlockSpec((tm, tn), lambda i, j: (i, j))],
        out_specs=pl.BlockSpec((tm, tn), lambda i, j: (i, j)),
        out_shape=jax.ShapeDtypeStruct((t, n), BF16),
        compiler_params=_params("parallel", "parallel"),
        name="glu",
    )(y, w, y)


def _merge_kernel(a_ref, wa_ref, s_ref, ws_ref, ga_ref, gs_ref, o_ref):
    up_a = _dot(a_ref[...], wa_ref[...])
    up_s = _dot(s_ref[...], ws_ref[...])
    o_ref[...] = (ga_ref[...].astype(F32) * up_a + gs_ref[...].astype(F32) * up_s).astype(o_ref.dtype)


def _merge(attn, wa, ssm, ws, gates):
    t, ka = attn.shape
    ks = ssm.shape[1]
    d = wa.shape[1]
    tm, tn = _tile(t, 1024), _tile(d, 1024)
    nj = d // tn
    return pl.pallas_call(
        _merge_kernel,
        grid=(t // tm, nj),
        in_specs=[pl.BlockSpec((tm, ka), lambda i, j: (i, 0)),
                  pl.BlockSpec((ka, tn), lambda i, j: (0, j)),
                  pl.BlockSpec((tm, ks), lambda i, j: (i, 0)),
                  pl.BlockSpec((ks, tn), lambda i, j: (0, j)),
                  pl.BlockSpec((tm, tn), lambda i, j: (i, j)),
                  pl.BlockSpec((tm, tn), lambda i, j: (i, j + nj))],
        out_specs=pl.BlockSpec((tm, tn), lambda i, j: (i, j)),
        out_shape=jax.ShapeDtypeStruct((t, d), BF16),
        compiler_params=_params("parallel", "parallel"),
        name="merge",
    )(attn, wa, ssm, ws, gates, gates)


def _resid_mm_kernel(x_ref, w_ref, h_ref, o_ref, *, nk):
    part = _dot(x_ref[...], w_ref[...])
    if nk == 1:
        o_ref[...] = h_ref[...] + part
    else:
        k = pl.program_id(2)

        @pl.when(k == 0)
        def _():
            o_ref[...] = h_ref[...] + part

        @pl.when(k > 0)
        def _():
            o_ref[...] += part


def _resid_matmul(x, w, h, *, name):
    t, k = x.shape
    n = w.shape[1]
    tk = k if k <= 4096 else _tile(k, 2048)
    nk = k // tk
    tm, tn = _tile(t, 1024), _tile(n, 512 if nk == 1 else 1024)
    return pl.pallas_call(
        functools.partial(_resid_mm_kernel, nk=nk),
        grid=(t // tm, n // tn, nk),
        in_specs=[pl.BlockSpec((tm, tk), lambda i, j, kk: (i, kk)),
                  pl.BlockSpec((tk, tn), lambda i, j, kk: (kk, j)),
                  pl.BlockSpec((tm, tn), lambda i, j, kk: (i, j))],
        out_specs=pl.BlockSpec((tm, tn), lambda i, j, kk: (i, j)),
        out_shape=jax.ShapeDtypeStruct((t, n), F32),
        input_output_aliases={2: 0},
        compiler_params=_params("parallel", "parallel", "arbitrary"),
        name=name,
    )(x, w, h)


def _ple_kernel(hn_ref, wg_ref, p_ref, wp_ref, h_ref, o_ref):
    gate = jax.nn.sigmoid(_dot(hn_ref[...], wg_ref[...]))
    emb = _dot(p_ref[...].astype(BF16), wp_ref[...])
    o_ref[...] = h_ref[...] + emb * gate


def _ple(hn, wg, p, wp, h):
    t, k = hn.shape
    kp = p.shape[1]
    n = wg.shape[1]
    tm, tn = _tile(t, 1024), _tile(n, 512)
    return pl.pallas_call(
        _ple_kernel,
        grid=(t // tm, n // tn),
        in_specs=[pl.BlockSpec((tm, k), lambda i, j: (i, 0)),
                  pl.BlockSpec((k, tn), lambda i, j: (0, j)),
                  pl.BlockSpec((tm, kp), lambda i, j: (i, 0)),
                  pl.BlockSpec((kp, tn), lambda i, j: (0, j)),
                  pl.BlockSpec((tm, tn), lambda i, j: (i, j))],
        out_specs=pl.BlockSpec((tm, tn), lambda i, j: (i, j)),
        out_shape=jax.ShapeDtypeStruct((t, n), F32),
        input_output_aliases={4: 0},
        compiler_params=_params("parallel", "parallel"),
        name="ple",
    )(hn, wg, p, wp, h)


def _attn_kernel(q_ref, k_ref, v_ref, tri_ref, o_ref, acc_ref, carry_ref, *, tq, scale):
    qi = pl.program_id(2)
    q = q_ref[...]
    tri = tri_ref[...]
    row = lax.broadcasted_iota(jnp.int32, (tq, tq), 0)
    col = lax.broadcasted_iota(jnp.int32, (tq, tq), 1)
    causal = col < row

    def key_block(kb, masked):
        start = pl.multiple_of(kb * tq, tq)
        k_blk = k_ref[pl.ds(start, tq), :]
        v_blk = v_ref[pl.ds(start, tq), :]
        z = lax.dot_general(q, k_blk, (((1,), (1,)), ((), ())),
                            preferred_element_type=F32) * scale
        log_1m = -(jnp.maximum(z, 0.0) + jnp.log1p(jnp.exp(-jnp.abs(z))))
        if masked:
            log_1m = jnp.where(causal, log_1m, 0.0)
        hi = log_1m.astype(BF16)
        lo = (log_1m - hi.astype(F32)).astype(BF16)
        tail = _dot(jnp.concatenate([hi, lo], axis=1), tri)
        w = jnp.exp(z + tail + carry_ref[...])
        if masked:
            w = jnp.where(causal, w, 0.0)
        acc_ref[...] += _dot(w.astype(BF16), v_blk)
        carry_ref[...] += tail[:, 0:1]

    acc_ref[...] = jnp.zeros_like(acc_ref)
    carry_ref[...] = jnp.zeros_like(carry_ref)
    key_block(qi, True)

    def body(i, c):
        key_block(qi - 1 - i, False)
        return c

    lax.fori_loop(0, qi, body, 0)
    o_ref[...] = acc_ref[...].astype(o_ref.dtype)


def _attention(proj, *, batch, seq, heads):
    tq = _tile(seq, 256)
    nq = seq // tq
    r = jnp.arange(tq)
    tri = (r[:, None] >= r[None, :]).astype(BF16)
    tri2 = jnp.concatenate([tri, tri], axis=0)
    return pl.pallas_call(
        functools.partial(_attn_kernel, tq=tq, scale=HEAD_DIM ** -0.5),
        grid=(batch, heads, nq),
        in_specs=[pl.BlockSpec((tq, HEAD_DIM), lambda b, h, i: (b * nq + i, h)),
                  pl.BlockSpec((seq, HEAD_DIM), lambda b, h, i: (b, heads + h)),
                  pl.BlockSpec((seq, HEAD_DIM), lambda b, h, i: (b, 2 * heads + h)),
                  pl.BlockSpec((2 * tq, tq), lambda b, h, i: (0, 0))],
        out_specs=pl.BlockSpec((tq, HEAD_DIM), lambda b, h, i: (b * nq + i, h)),
        out_shape=jax.ShapeDtypeStruct((batch * seq, heads * HEAD_DIM), BF16),
        scratch_shapes=[pltpu.VMEM((tq, HEAD_DIM), F32), pltpu.VMEM((tq, 1), F32)],
        compiler_params=_params("parallel", "parallel", "parallel"),
        name="stickbreak_attn",
    )(proj, proj, proj, tri2)


def _ssm_kernel(u_ref, b_ref, m_ref, c_ref, d_ref, o_ref, x_ref, *, seq):
    ns = SLAB_STATES
    u = u_ref[...]
    x_ref[...] = _dot(u, b_ref[0])

    def step(i, carry):
        c_re, c_im = carry
        r0 = pl.multiple_of(i * SUBLANES, SUBLANES)
        x_re = x_ref[pl.ds(r0, SUBLANES), 0:ns]
        x_im = x_ref[pl.ds(r0, SUBLANES), ns:2 * ns]
        for n, shift in enumerate((1, 2, 4)):
            a_re = m_ref[0, 2 * n]
            a_im = m_ref[0, 2 * n + 1]
            s_re = pltpu.roll(x_re, shift, 0)
            s_im = pltpu.roll(x_im, shift, 0)
            x_re, x_im = (x_re + (a_re * s_re - a_im * s_im),
                          x_im + (a_re * s_im + a_im * s_re))
        p_re = m_ref[0, 6]
        p_im = m_ref[0, 7]
        x_re, x_im = (x_re + (p_re * c_re - p_im * c_im),
                      x_im + (p_re * c_im + p_im * c_re))
        x_ref[pl.ds(r0, SUBLANES), 0:ns] = x_re
        x_ref[pl.ds(r0, SUBLANES), ns:2 * ns] = x_im
        last = SUBLANES - 1
        return (jnp.broadcast_to(x_re[last:last + 1, :], (SUBLANES, ns)),
                jnp.broadcast_to(x_im[last:last + 1, :], (SUBLANES, ns)))

    zero = jnp.zeros((SUBLANES, ns), F32)
    lax.fori_loop(0, seq // SUBLANES, step, (zero, zero))

    y = _dot(x_ref[...].astype(BF16), c_ref[0]) + d_ref[...] * u.astype(F32)
    o_ref[...] = jax.nn.gelu(y, approximate=True).astype(o_ref.dtype)


def _ssm_operands(lam_re, lam_im, log_dt, b_re, b_im, c_re, c_im):
    g = lam_re.shape[0]
    nslab = g // SLAB_GROUPS
    dt = jnp.exp(log_dt)[:, None]
    lam = lax.complex(lam_re, lam_im)
    lam_bar = jnp.exp(lam * dt)
    b_bar = ((lam_bar - 1.0) / lam)[..., None] * lax.complex(b_re, b_im)
    eye = jnp.eye(SLAB_GROUPS, dtype=F32)

    def b_block(part):
        t = part.reshape(nslab, SLAB_GROUPS, SSM_STATE, SSM_GROUP).transpose(0, 1, 3, 2)
        return jnp.einsum("sgcp,gh->sgchp", t, eye).reshape(nslab, SLAB, SLAB_STATES)

    def c_block(part):
        t = part.reshape(nslab, SLAB_GROUPS, SSM_GROUP, SSM_STATE).transpose(0, 1, 3, 2)
        return jnp.einsum("sgpc,gh->sgphc", t, eye).reshape(nslab, SLAB_STATES, SLAB)

    b_mat = jnp.concatenate([b_block(b_bar.real), b_block(b_bar.imag)], axis=2).astype(BF16)
    c_mat = jnp.concatenate([c_block(c_re), c_block(-c_im)], axis=1).astype(BF16)

    lam1 = lam_bar.reshape(nslab, 1, SLAB_STATES)
    lam2 = lam1 * lam1
    lam4 = lam2 * lam2
    rows = jnp.arange(SUBLANES)[None, :, None]
    planes = []
    for lam_pow, shift in ((lam1, 1), (lam2, 2), (lam4, 4)):
        a = jnp.where(rows >= shift, lam_pow, 0.0)
        planes += [a.real, a.imag]
    pows = [lam1]
    for _ in range(SUBLANES - 1):
        pows.append(pows[-1] * lam1)
    p = jnp.concatenate(pows, axis=1)
    planes += [p.real, p.imag]
    mult = jnp.stack(planes, axis=1).astype(F32)
    return b_mat, mult, c_mat


def _ssm(proj, u_col0, b_mat, mult, c_mat, d_skip, *, batch, seq):
    nslab = b_mat.shape[0]
    width = nslab * SLAB
    assert u_col0 % SLAB == 0
    joff = u_col0 // SLAB
    return pl.pallas_call(
        functools.partial(_ssm_kernel, seq=seq),
        grid=(batch, nslab),
        in_specs=[pl.BlockSpec((seq, SLAB), lambda b, s: (b, s + joff)),
                  pl.BlockSpec((1, SLAB, 2 * SLAB_STATES), lambda b, s: (s, 0, 0)),
                  pl.BlockSpec((1, 8, SUBLANES, SLAB_STATES), lambda b, s: (s, 0, 0, 0)),
                  pl.BlockSpec((1, 2 * SLAB_STATES, SLAB), lambda b, s: (s, 0, 0)),
                  pl.BlockSpec((1, SLAB), lambda b, s: (0, s))],
        out_specs=pl.BlockSpec((seq, SLAB), lambda b, s: (b, s)),
        out_shape=jax.ShapeDtypeStruct((batch * seq, width), BF16),
        scratch_shapes=[pltpu.VMEM((seq, 2 * SLAB_STATES), F32)],
        compiler_params=_params("parallel", "parallel"),
        name="s5_ssm",
    )(proj, b_mat, mult, c_mat, d_skip.reshape(1, width))


def kernel(x, p, g_mix, w_in, w_br_attn, lam_re, lam_im, log_dt, b_re, b_im, c_re, c_im, d_skip, w_glu, w_br_ssm, w_o, g_mlp, w_ff1, w_ff2, g_ple, w_ple_gate, w_ple, g_final):
    bsz, seq, d = x.shape
    depth = w_in.shape[0]
    attn_w = w_br_attn.shape[1]
    ssm_w = w_br_ssm.shape[1]
    heads = attn_w // HEAD_DIM
    t = bsz * seq
    assert w_in.shape[2] == 3 * attn_w + ssm_w + 2 * d

    h = x.reshape(t, d)
    for i in range(depth):
        w_in_i = w_in[i].astype(BF16)
        xn = _rmsnorm(h, g_mix[i], BF16)
        mix_w = 3 * attn_w + ssm_w
        proj = _matmul(xn, w_in_i, col0=0, ncols=mix_w, act=None, name="proj_mix")
        gates = _matmul(xn, w_in_i, col0=mix_w, ncols=2 * d, act="sigmoid", name="proj_gates")

        attn = _attention(proj, batch=bsz, seq=seq, heads=heads)
        b_mat, mult, c_mat = _ssm_operands(lam_re[i], lam_im[i], log_dt[i], b_re[i], b_im[i],
                                           c_re[i], c_im[i])
        y = _ssm(proj, 3 * attn_w, b_mat, mult, c_mat, d_skip[i], batch=bsz, seq=seq)
        ssm = _glu(y, w_glu[i].astype(BF16))
        merged = _merge(attn, w_br_attn[i].astype(BF16), ssm, w_br_ssm[i].astype(BF16), gates)
        h = _resid_matmul(merged, w_o[i].astype(BF16), h, name="out_proj")

        hn = _rmsnorm(h, g_mlp[i], BF16)
        act = _matmul(hn, w_ff1[i].astype(BF16), col0=0, ncols=w_ff1.shape[2], act="relu2", name="ff1")
        h = _resid_matmul(act, w_ff2[i].astype(BF16), h, name="ff2")

        hp = _rmsnorm(h, g_ple[i], BF16)
        h = _ple(hp, w_ple_gate[i].astype(BF16), p[i].reshape(t, -1), w_ple[i].astype(BF16), h)
    return _rmsnorm(h, g_final, x.dtype).reshape(bsz, seq, d)
```

```python
import functools
import math

import jax
import jax.numpy as jnp
from jax import lax
from jax.experimental import pallas as pl
from jax.experimental.pallas import tpu as pltpu

F32 = jnp.float32
BF16 = jnp.bfloat16

RMS_EPS = 1e-6
HEAD_DIM = 128
ATTN_Q_SCALE = HEAD_DIM ** -0.5 * math.log2(math.e)
ATTN_HEADS_PER_STEP = 8
SSM_GROUP = 16
SSM_STATE = 64
SLAB = 256
SLAB_GROUPS = SLAB // SSM_GROUP
SLAB_STATES = SLAB_GROUPS * SSM_STATE
SUBLANES = 8
V7X_VMEM_LIMIT = 56 * 1024 * 1024


def _tile(dim, pref):
    if dim % pref == 0:
        return pref
    assert dim < pref, (dim, pref)
    return dim


def _params(*sem):
    return pltpu.CompilerParams(dimension_semantics=sem, vmem_limit_bytes=V7X_VMEM_LIMIT)


def _dot(a, b):
    return jnp.dot(a, b, preferred_element_type=F32)


def _rmsnorm_kernel(x_ref, g_ref, o_ref):
    x = x_ref[...]
    ms = jnp.mean(x * x, axis=-1, keepdims=True)
    o_ref[...] = (x * lax.rsqrt(ms + RMS_EPS) * g_ref[...]).astype(o_ref.dtype)


def _rmsnorm(x, g, out_dtype):
    t, d = x.shape
    tm = _tile(t, 256)
    return pl.pallas_call(
        _rmsnorm_kernel,
        grid=(t // tm,),
        in_specs=[pl.BlockSpec((tm, d), lambda i: (i, 0)),
                  pl.BlockSpec((1, d), lambda i: (0, 0))],
        out_specs=pl.BlockSpec((tm, d), lambda i: (i, 0)),
        out_shape=jax.ShapeDtypeStruct((t, d), out_dtype),
        compiler_params=_params("parallel"),
        name="rmsnorm",
    )(x, g.reshape(1, d))


def _mm_kernel(x_ref, w_ref, s_ref, o_ref, *, act):
    acc = _dot(x_ref[...], w_ref[...])
    if act == "sigmoid":
        acc = jax.nn.sigmoid(acc)
    elif act == "relu2":
        acc = jnp.square(jnp.maximum(acc, 0.0))
    elif act == "colscale":
        acc = acc * s_ref[...]
    o_ref[...] = acc.astype(o_ref.dtype)


def _matmul(x, w, *, col0, ncols, act, name, col_scale=None):
    t, k = x.shape
    tm, tn = _tile(t, 1024), _tile(ncols, 1024)
    assert col0 % tn == 0
    assert (col_scale is not None) == (act == "colscale")
    joff = col0 // tn
    if col_scale is None:
        col_scale = jnp.ones((ncols,), F32)
    return pl.pallas_call(
        functools.partial(_mm_kernel, act=act),
        grid=(t // tm, ncols // tn),
        in_specs=[pl.BlockSpec((tm, k), lambda i, j: (i, 0)),
                  pl.BlockSpec((k, tn), lambda i, j: (0, j + joff)),
                  pl.BlockSpec((1, tn), lambda i, j: (0, j))],
        out_specs=pl.BlockSpec((tm, tn), lambda i, j: (i, j)),
        out_shape=jax.ShapeDtypeStruct((t, ncols), BF16),
        compiler_params=_params("parallel", "parallel"),
        name=name,
    )(x, w, col_scale.reshape(1, ncols))


def _glu_kernel(y_ref, w_ref, yt_ref, o_ref):
    gate = jax.nn.sigmoid(_dot(y_ref[...], w_ref[...]))
    o_ref[...] = (yt_ref[...].astype(F32) * gate).astype(o_ref.dtype)


def _glu(y, w):
    t, k = y.shape
    n = w.shape[1]
    tm, tn = _tile(t, 1024), _tile(n, 1024)
    return pl.pallas_call(
        _glu_kernel,
        grid=(t // tm, n // tn),
        in_specs=[pl.BlockSpec((tm, k), lambda i, j: (i, 0)),
                  pl.BlockSpec((k, tn), lambda i, j: (0, j)),
                  pl.BlockSpec((tm, tn), lambda i, j: (i, j))],
        out_specs=pl.BlockSpec((tm, tn), lambda i, j: (i, j)),
        out_shape=jax.ShapeDtypeStruct((t, n), BF16),
        compiler_params=_params("parallel", "parallel"),
        name="glu",
    )(y, w, y)


def _merge_kernel(a_ref, wa_ref, s_ref, ws_ref, ga_ref, gs_ref, o_ref):
    up_a = _dot(a_ref[...], wa_ref[...])
    up_s = _dot(s_ref[...], ws_ref[...])
    o_ref[...] = (ga_ref[...].astype(F32) * up_a + gs_ref[...].astype(F32) * up_s).astype(o_ref.dtype)


def _merge(attn, wa, ssm, ws, gates):
    t, ka = attn.shape
    ks = ssm.shape[1]
    d = wa.shape[1]
    tm, tn = _tile(t, 1024), _tile(d, 1024)
    nj = d // tn
    return pl.pallas_call(
        _merge_kernel,
        grid=(t // tm, nj),
        in_specs=[pl.BlockSpec((tm, ka), lambda i, j: (i, 0)),
                  pl.BlockSpec((ka, tn), lambda i, j: (0, j)),
                  pl.BlockSpec((tm, ks), lambda i, j: (i, 0)),
                  pl.BlockSpec((ks, tn), lambda i, j: (0, j)),
                  pl.BlockSpec((tm, tn), lambda i, j: (i, j)),
                  pl.BlockSpec((tm, tn), lambda i, j: (i, j + nj))],
        out_specs=pl.BlockSpec((tm, tn), lambda i, j: (i, j)),
        out_shape=jax.ShapeDtypeStruct((t, d), BF16),
        compiler_params=_params("parallel", "parallel"),
        name="merge",
    )(attn, wa, ssm, ws, gates, gates)


def _resid_mm_kernel(x_ref, w_ref, h_ref, o_ref, *, nk):
    part = _dot(x_ref[...], w_ref[...])
    if nk == 1:
        o_ref[...] = h_ref[...] + part
    else:
        k = pl.program_id(2)

        @pl.when(k == 0)
        def _():
            o_ref[...] = h_ref[...] + part

        @pl.when(k > 0)
        def _():
            o_ref[...] += part


def _resid_matmul(x, w, h, *, name):
    t, k = x.shape
    n = w.shape[1]
    tk = k if k <= 4096 else _tile(k, 2048)
    nk = k // tk
    tm, tn = _tile(t, 1024), _tile(n, 512 if nk == 1 else 1024)
    return pl.pallas_call(
        functools.partial(_resid_mm_kernel, nk=nk),
        grid=(t // tm, n // tn, nk),
        in_specs=[pl.BlockSpec((tm, tk), lambda i, j, kk: (i, kk)),
                  pl.BlockSpec((tk, tn), lambda i, j, kk: (kk, j)),
                  pl.BlockSpec((tm, tn), lambda i, j, kk: (i, j))],
        out_specs=pl.BlockSpec((tm, tn), lambda i, j, kk: (i, j)),
        out_shape=jax.ShapeDtypeStruct((t, n), F32),
        input_output_aliases={2: 0},
        compiler_params=_params("parallel", "parallel", "arbitrary"),
        name=name,
    )(x, w, h)


def _ple_kernel(hn_ref, wg_ref, p_ref, wp_ref, h_ref, o_ref):
    gate = jax.nn.sigmoid(_dot(hn_ref[...], wg_ref[...]))
    emb = _dot(p_ref[...].astype(BF16), wp_ref[...])
    o_ref[...] = h_ref[...] + emb * gate


def _ple(hn, wg, p, wp, h):
    t, k = hn.shape
    kp = p.shape[1]
    n = wg.shape[1]
    tm, tn = _tile(t, 1024), _tile(n, 512)
    return pl.pallas_call(
        _ple_kernel,
        grid=(t // tm, n // tn),
        in_specs=[pl.BlockSpec((tm, k), lambda i, j: (i, 0)),
                  pl.BlockSpec((k, tn), lambda i, j: (0, j)),
                  pl.BlockSpec((tm, kp), lambda i, j: (i, 0)),
                  pl.BlockSpec((kp, tn), lambda i, j: (0, j)),
                  pl.BlockSpec((tm, tn), lambda i, j: (i, j))],
        out_specs=pl.BlockSpec((tm, tn), lambda i, j: (i, j)),
        out_shape=jax.ShapeDtypeStruct((t, n), F32),
        input_output_aliases={4: 0},
        compiler_params=_params("parallel", "parallel"),
        name="ple",
    )(hn, wg, p, wp, h)


def _attn_kernel(q_ref, k_ref, v_ref, tri_ref, o_ref, acc_ref, carry_ref, *, tq, nheads):
    qi = pl.program_id(2)
    tri = tri_ref[...]
    row = lax.broadcasted_iota(jnp.int32, (tq, tq), 0)
    col = lax.broadcasted_iota(jnp.int32, (tq, tq), 1)
    causal = col < row

    def key_block(kb, masked):
        start = pl.multiple_of(kb * tq, tq)
        heads = range(nheads)
        lanes = [slice(hd * HEAD_DIM, (hd + 1) * HEAD_DIM) for hd in heads]
        zs = [lax.dot_general(q_ref[:, lanes[hd]], k_ref[pl.ds(start, tq), lanes[hd]],
                              (((1,), (1,)), ((), ())), preferred_element_type=F32) for hd in heads]
        tails = []
        for z in zs:
            neg_abs = lax.bitcast_convert_type(
                lax.bitcast_convert_type(z, jnp.uint32) | jnp.uint32(0x80000000), F32)
            softplus = jnp.maximum(z, 0.0) + jnp.log2(1.0 + jnp.exp2(neg_abs))
            if masked:
                softplus = jnp.where(causal, softplus, 0.0)
            hi = softplus.astype(BF16)
            lo = (softplus - hi.astype(F32)).astype(BF16)
            tails.append(_dot(jnp.concatenate([hi, lo], axis=1), tri))
        for hd in heads:
            w = jnp.exp2(zs[hd] + tails[hd] + carry_ref[hd])
            if masked:
                w = jnp.where(causal, w, 0.0)
            acc_ref[:, lanes[hd]] += _dot(w.astype(BF16), v_ref[pl.ds(start, tq), lanes[hd]])
            carry_ref[hd] += tails[hd][:, 0:1]

    acc_ref[...] = jnp.zeros_like(acc_ref)
    carry_ref[...] = jnp.zeros_like(carry_ref)
    key_block(qi, True)

    def body(i, c):
        key_block(qi - 1 - i, False)
        return c

    lax.fori_loop(0, qi, body, 0)
    o_ref[...] = acc_ref[...].astype(o_ref.dtype)


def _attention(proj, *, batch, seq, heads):
    tq = _tile(seq, 256)
    nq = seq // tq
    nheads = _tile(heads, ATTN_HEADS_PER_STEP)
    hgroups = heads // nheads
    width = nheads * HEAD_DIM
    r = jnp.arange(tq)
    tri = -(r[:, None] >= r[None, :]).astype(BF16)
    tri2 = jnp.concatenate([tri, tri], axis=0)
    return pl.pallas_call(
        functools.partial(_attn_kernel, tq=tq, nheads=nheads),
        grid=(batch, hgroups, nq),
        in_specs=[pl.BlockSpec((tq, width), lambda b, h, i: (b * nq + i, h)),
                  pl.BlockSpec((seq, width), lambda b, h, i: (b, hgroups + h)),
                  pl.BlockSpec((seq, width), lambda b, h, i: (b, 2 * hgroups + h)),
                  pl.BlockSpec((2 * tq, tq), lambda b, h, i: (0, 0))],
        out_specs=pl.BlockSpec((tq, width), lambda b, h, i: (b * nq + i, h)),
        out_shape=jax.ShapeDtypeStruct((batch * seq, heads * HEAD_DIM), BF16),
        scratch_shapes=[pltpu.VMEM((tq, width), F32), pltpu.VMEM((nheads, tq, 1), F32)],
        compiler_params=_params("parallel", "parallel", "parallel"),
        name="stickbreak_attn",
    )(proj, proj, proj, tri2)


def _ssm_kernel(u_ref, b_ref, m_ref, c_ref, d_ref, o_ref, x_ref, *, seq):
    ns = SLAB_STATES
    u = u_ref[...]
    x_ref[...] = _dot(u, b_ref[0])

    def step(i, carry):
        c_re, c_im = carry
        r0 = pl.multiple_of(i * SUBLANES, SUBLANES)
        x_re = x_ref[pl.ds(r0, SUBLANES), 0:ns]
        x_im = x_ref[pl.ds(r0, SUBLANES), ns:2 * ns]
        for n, shift in enumerate((1, 2, 4)):
            a_re = m_ref[0, 2 * n]
            a_im = m_ref[0, 2 * n + 1]
            s_re = pltpu.roll(x_re, shift, 0)
            s_im = pltpu.roll(x_im, shift, 0)
            x_re, x_im = (x_re + (a_re * s_re - a_im * s_im),
                          x_im + (a_re * s_im + a_im * s_re))
        p_re = m_ref[0, 6]
        p_im = m_ref[0, 7]
        x_re, x_im = (x_re + (p_re * c_re - p_im * c_im),
                      x_im + (p_re * c_im + p_im * c_re))
        x_ref[pl.ds(r0, SUBLANES), 0:ns] = x_re
        x_ref[pl.ds(r0, SUBLANES), ns:2 * ns] = x_im
        last = SUBLANES - 1
        return (jnp.broadcast_to(x_re[last:last + 1, :], (SUBLANES, ns)),
                jnp.broadcast_to(x_im[last:last + 1, :], (SUBLANES, ns)))

    zero = jnp.zeros((SUBLANES, ns), F32)
    lax.fori_loop(0, seq // SUBLANES, step, (zero, zero))

    y = _dot(x_ref[...].astype(BF16), c_ref[0]) + d_ref[...] * u.astype(F32)
    o_ref[...] = jax.nn.gelu(y, approximate=True).astype(o_ref.dtype)


def _ssm_operands(lam_re, lam_im, log_dt, b_re, b_im, c_re, c_im):
    g = lam_re.shape[0]
    nslab = g // SLAB_GROUPS
    dt = jnp.exp(log_dt)[:, None]
    lam = lax.complex(lam_re, lam_im)
    lam_bar = jnp.exp(lam * dt)
    b_bar = ((lam_bar - 1.0) / lam)[..., None] * lax.complex(b_re, b_im)
    eye = jnp.eye(SLAB_GROUPS, dtype=F32)

    def b_block(part):
        t = part.reshape(nslab, SLAB_GROUPS, SSM_STATE, SSM_GROUP).transpose(0, 1, 3, 2)
        return jnp.einsum("sgcp,gh->sgchp", t, eye).reshape(nslab, SLAB, SLAB_STATES)

    def c_block(part):
        t = part.reshape(nslab, SLAB_GROUPS, SSM_GROUP, SSM_STATE).transpose(0, 1, 3, 2)
        return jnp.einsum("sgpc,gh->sgphc", t, eye).reshape(nslab, SLAB_STATES, SLAB)

    b_mat = jnp.concatenate([b_block(b_bar.real), b_block(b_bar.imag)], axis=2).astype(BF16)
    c_mat = jnp.concatenate([c_block(c_re), c_block(-c_im)], axis=1).astype(BF16)

    lam1 = lam_bar.reshape(nslab, 1, SLAB_STATES)
    lam2 = lam1 * lam1
    lam4 = lam2 * lam2
    rows = jnp.arange(SUBLANES)[None, :, None]
    planes = []
    for lam_pow, shift in ((lam1, 1), (lam2, 2), (lam4, 4)):
        a = jnp.where(rows >= shift, lam_pow, 0.0)
        planes += [a.real, a.imag]
    pows = [lam1]
    for _ in range(SUBLANES - 1):
        pows.append(pows[-1] * lam1)
    p = jnp.concatenate(pows, axis=1)
    planes += [p.real, p.imag]
    mult = jnp.stack(planes, axis=1).astype(F32)
    return b_mat, mult, c_mat


def _ssm(proj, u_col0, b_mat, mult, c_mat, d_skip, *, batch, seq):
    nslab = b_mat.shape[0]
    width = nslab * SLAB
    assert u_col0 % SLAB == 0
    joff = u_col0 // SLAB
    return pl.pallas_call(
        functools.partial(_ssm_kernel, seq=seq),
        grid=(batch, nslab),
        in_specs=[pl.BlockSpec((seq, SLAB), lambda b, s: (b, s + joff)),
                  pl.BlockSpec((1, SLAB, 2 * SLAB_STATES), lambda b, s: (s, 0, 0)),
                  pl.BlockSpec((1, 8, SUBLANES, SLAB_STATES), lambda b, s: (s, 0, 0, 0)),
                  pl.BlockSpec((1, 2 * SLAB_STATES, SLAB), lambda b, s: (s, 0, 0)),
                  pl.BlockSpec((1, SLAB), lambda b, s: (0, s))],
        out_specs=pl.BlockSpec((seq, SLAB), lambda b, s: (b, s)),
        out_shape=jax.ShapeDtypeStruct((batch * seq, width), BF16),
        scratch_shapes=[pltpu.VMEM((seq, 2 * SLAB_STATES), F32)],
        compiler_params=_params("parallel", "parallel"),
        name="s5_ssm",
    )(proj, b_mat, mult, c_mat, d_skip.reshape(1, width))


def kernel(x, p, g_mix, w_in, w_br_attn, lam_re, lam_im, log_dt, b_re, b_im, c_re, c_im, d_skip, w_glu, w_br_ssm, w_o, g_mlp, w_ff1, w_ff2, g_ple, w_ple_gate, w_ple, g_final):
    bsz, seq, d = x.shape
    depth = w_in.shape[0]
    attn_w = w_br_attn.shape[1]
    ssm_w = w_br_ssm.shape[1]
    heads = attn_w // HEAD_DIM
    t = bsz * seq
    assert w_in.shape[2] == 3 * attn_w + ssm_w + 2 * d

    h = x.reshape(t, d)
    for i in range(depth):
        w_in_i = w_in[i].astype(BF16)
        xn = _rmsnorm(h, g_mix[i], BF16)
        mix_w = 3 * attn_w + ssm_w
        q_scale = jnp.where(jnp.arange(mix_w) < attn_w, ATTN_Q_SCALE, 1.0).astype(F32)
        proj = _matmul(xn, w_in_i, col0=0, ncols=mix_w, act="colscale", col_scale=q_scale,
                       name="proj_mix")
        gates = _matmul(xn, w_in_i, col0=mix_w, ncols=2 * d, act="sigmoid", name="proj_gates")

        attn = _attention(proj, batch=bsz, seq=seq, heads=heads)
        b_mat, mult, c_mat = _ssm_operands(lam_re[i], lam_im[i], log_dt[i], b_re[i], b_im[i],
                                           c_re[i], c_im[i])
        y = _ssm(proj, 3 * attn_w, b_mat, mult, c_mat, d_skip[i], batch=bsz, seq=seq)
        ssm = _glu(y, w_glu[i].astype(BF16))
        merged = _merge(attn, w_br_attn[i].astype(BF16), ssm, w_br_ssm[i].astype(BF16), gates)
        h = _resid_matmul(merged, w_o[i].astype(BF16), h, name="out_proj")

        hn = _rmsnorm(h, g_mlp[i], BF16)
        act = _matmul(hn, w_ff1[i].astype(BF16), col0=0, ncols=w_ff1.shape[2], act="relu2", name="ff1")
        h = _resid_matmul(act, w_ff2[i].astype(BF16), h, name="ff2")

        hp = _rmsnorm(h, g_ple[i], BF16)
        h = _ple(hp, w_ple_gate[i].astype(BF16), p[i].reshape(t, -1), w_ple[i].astype(BF16), h)
    return _rmsnorm(h, g_final, x.dtype).reshape(bsz, seq, d)
```

```python
import functools
import math

import jax
import jax.numpy as jnp
from jax import lax
from jax.experimental import pallas as pl
from jax.experimental.pallas import tpu as pltpu

F32 = jnp.float32
BF16 = jnp.bfloat16

RMS_EPS = 1e-6
HEAD_DIM = 128
ATTN_Q_SCALE = HEAD_DIM ** -0.5 * math.log2(math.e)
ATTN_HEADS_PER_STEP = 8
SSM_GROUP = 16
SSM_STATE = 64
SLAB = 256
SLAB_GROUPS = SLAB // SSM_GROUP
SLAB_STATES = SLAB_GROUPS * SSM_STATE
SUBLANES = 8
V7X_VMEM_LIMIT = 56 * 1024 * 1024
CAST_BLOCK_BYTES = 8 * 1024 * 1024


def _tile(dim, pref):
    if dim % pref == 0:
        return pref
    assert dim < pref, (dim, pref)
    return dim


def _params(*sem):
    return pltpu.CompilerParams(dimension_semantics=sem, vmem_limit_bytes=V7X_VMEM_LIMIT)


def _dot(a, b):
    return jnp.dot(a, b, preferred_element_type=F32)


def _wspec(layer, k, tn, joff=0):
    return pl.BlockSpec((None, k, tn), lambda i, j, *kk: (layer, kk[0] if kk else 0, j + joff))


def _rstd(ss_ref, d):
    return lax.rsqrt(ss_ref[...] * (1.0 / d) + RMS_EPS)


def _emit_stream(h, h_ref, hb_ref, ss_ref, j):
    h_ref[...] = h
    hb_ref[...] = h.astype(BF16)
    row_ss = jnp.sum(h * h, axis=1, keepdims=True)

    @pl.when(j == 0)
    def _():
        ss_ref[...] = row_ss

    @pl.when(j > 0)
    def _():
        ss_ref[...] += row_ss


def _stream_out(t, n, tm, tn):
    specs = [pl.BlockSpec((tm, tn), lambda i, j, *kk: (i, j)),
             pl.BlockSpec((tm, tn), lambda i, j, *kk: (i, j)),
             pl.BlockSpec((tm, 1), lambda i, j, *kk: (i, 0))]
    shapes = [jax.ShapeDtypeStruct((t, n), F32), jax.ShapeDtypeStruct((t, n), BF16),
              jax.ShapeDtypeStruct((t, 1), F32)]
    return specs, shapes


def _cast_kernel(w_ref, g_ref, o_ref, *, scaled):
    w = w_ref[...]
    if scaled:
        w = w * g_ref[...]
    o_ref[...] = w.astype(o_ref.dtype)


def _to_bf16(w, gain=None):
    depth, k, n = w.shape
    bk = max(16, min(k, CAST_BLOCK_BYTES // (4 * n)))
    assert k % bk == 0, (k, bk)
    scaled = gain is not None
    if not scaled:
        gain = jnp.ones((depth, k), F32)
    spec = pl.BlockSpec((None, bk, n), lambda a, r: (a, r, 0))
    return pl.pallas_call(
        functools.partial(_cast_kernel, scaled=scaled),
        grid=(depth, k // bk),
        in_specs=[spec, pl.BlockSpec((None, bk, 1), lambda a, r: (a, r, 0))],
        out_specs=spec,
        out_shape=jax.ShapeDtypeStruct(w.shape, BF16),
        compiler_params=_params("parallel", "parallel"),
        name="cast_bf16",
    )(w, gain.reshape(depth, k, 1))


def _enter_kernel(x_ref, xb_ref, ss_ref):
    x = x_ref[...]
    xb_ref[...] = x.astype(BF16)
    ss_ref[...] = jnp.sum(x * x, axis=1, keepdims=True)


def _enter(x):
    t, d = x.shape
    tm = _tile(t, 256)
    return pl.pallas_call(
        _enter_kernel,
        grid=(t // tm,),
        in_specs=[pl.BlockSpec((tm, d), lambda i: (i, 0))],
        out_specs=[pl.BlockSpec((tm, d), lambda i: (i, 0)), pl.BlockSpec((tm, 1), lambda i: (i, 0))],
        out_shape=[jax.ShapeDtypeStruct((t, d), BF16), jax.ShapeDtypeStruct((t, 1), F32)],
        compiler_params=_params("parallel"),
        name="enter",
    )(x)


def _rmsnorm_kernel(x_ref, g_ref, o_ref):
    x = x_ref[...]
    ms = jnp.mean(x * x, axis=-1, keepdims=True)
    o_ref[...] = (x * lax.rsqrt(ms + RMS_EPS) * g_ref[...]).astype(o_ref.dtype)


def _rmsnorm(x, g, out_dtype):
    t, d = x.shape
    tm = _tile(t, 256)
    return pl.pallas_call(
        _rmsnorm_kernel,
        grid=(t // tm,),
        in_specs=[pl.BlockSpec((tm, d), lambda i: (i, 0)),
                  pl.BlockSpec((1, d), lambda i: (0, 0))],
        out_specs=pl.BlockSpec((tm, d), lambda i: (i, 0)),
        out_shape=jax.ShapeDtypeStruct((t, d), out_dtype),
        compiler_params=_params("parallel"),
        name="rmsnorm",
    )(x, g.reshape(1, d))


def _normed_mm_kernel(hb_ref, w_ref, ss_ref, s_ref, o_ref, *, act, d):
    acc = _dot(hb_ref[...], w_ref[...]) * _rstd(ss_ref, d)
    if act == "sigmoid":
        acc = jax.nn.sigmoid(acc)
    elif act == "relu2":
        acc = jnp.square(jnp.maximum(acc, 0.0))
    elif act == "colscale":
        acc = acc * s_ref[...]
    o_ref[...] = acc.astype(o_ref.dtype)


def _normed_matmul(hb, ss, w, layer, *, col0, ncols, act, name, col_scale=None):
    t, k = hb.shape
    tm, tn = _tile(t, 1024), _tile(ncols, 1024)
    assert col0 % tn == 0
    assert (col_scale is not None) == (act == "colscale")
    if col_scale is None:
        col_scale = jnp.ones((ncols,), F32)
    return pl.pallas_call(
        functools.partial(_normed_mm_kernel, act=act, d=k),
        grid=(t // tm, ncols // tn),
        in_specs=[pl.BlockSpec((tm, k), lambda i, j: (i, 0)),
                  _wspec(layer, k, tn, col0 // tn),
                  pl.BlockSpec((tm, 1), lambda i, j: (i, 0)),
                  pl.BlockSpec((1, tn), lambda i, j: (0, j))],
        out_specs=pl.BlockSpec((tm, tn), lambda i, j: (i, j)),
        out_shape=jax.ShapeDtypeStruct((t, ncols), BF16),
        compiler_params=_params("parallel", "parallel"),
        name=name,
    )(hb, w, ss, col_scale.reshape(1, ncols))


def _glu_kernel(y_ref, w_ref, yt_ref, o_ref):
    gate = jax.nn.sigmoid(_dot(y_ref[...], w_ref[...]))
    o_ref[...] = (yt_ref[...].astype(F32) * gate).astype(o_ref.dtype)


def _glu(y, w, layer):
    t, k = y.shape
    n = w.shape[2]
    tm, tn = _tile(t, 1024), _tile(n, 1024)
    return pl.pallas_call(
        _glu_kernel,
        grid=(t // tm, n // tn),
        in_specs=[pl.BlockSpec((tm, k), lambda i, j: (i, 0)),
                  _wspec(layer, k, tn),
                  pl.BlockSpec((tm, tn), lambda i, j: (i, j))],
        out_specs=pl.BlockSpec((tm, tn), lambda i, j: (i, j)),
        out_shape=jax.ShapeDtypeStruct((t, n), BF16),
        compiler_params=_params("parallel", "parallel"),
        name="glu",
    )(y, w, y)


def _merge_kernel(a_ref, wa_ref, s_ref, ws_ref, ga_ref, gs_ref, o_ref):
    up_a = _dot(a_ref[...], wa_ref[...])
    up_s = _dot(s_ref[...], ws_ref[...])
    o_ref[...] = (ga_ref[...].astype(F32) * up_a + gs_ref[...].astype(F32) * up_s).astype(o_ref.dtype)


def _merge(attn, wa, ssm, ws, gates, layer):
    t, ka = attn.shape
    ks = ssm.shape[1]
    d = wa.shape[2]
    tm, tn = _tile(t, 1024), _tile(d, 1024)
    nj = d // tn
    return pl.pallas_call(
        _merge_kernel,
        grid=(t // tm, nj),
        in_specs=[pl.BlockSpec((tm, ka), lambda i, j: (i, 0)),
                  _wspec(layer, ka, tn),
                  pl.BlockSpec((tm, ks), lambda i, j: (i, 0)),
                  _wspec(layer, ks, tn),
                  pl.BlockSpec((tm, tn), lambda i, j: (i, j)),
                  pl.BlockSpec((tm, tn), lambda i, j: (i, j + nj))],
        out_specs=pl.BlockSpec((tm, tn), lambda i, j: (i, j)),
        out_shape=jax.ShapeDtypeStruct((t, d), BF16),
        compiler_params=_params("parallel", "parallel"),
        name="merge",
    )(attn, wa, ssm, ws, gates, gates)


def _resid_mm_kernel(x_ref, w_ref, h_ref, o_ref, ob_ref, ss_ref, *, nk):
    j = pl.program_id(1)
    if nk == 1:
        _emit_stream(h_ref[...] + _dot(x_ref[...], w_ref[...]), o_ref, ob_ref, ss_ref, j)
        return
    k = pl.program_id(2)

    @pl.when(k == 0)
    def _():
        o_ref[...] = h_ref[...] + _dot(x_ref[...], w_ref[...])

    @pl.when(jnp.logical_and(k > 0, k < nk - 1))
    def _():
        o_ref[...] += _dot(x_ref[...], w_ref[...])

    @pl.when(k == nk - 1)
    def _():
        _emit_stream(o_ref[...] + _dot(x_ref[...], w_ref[...]), o_ref, ob_ref, ss_ref, j)


def _resid_matmul(x, w, layer, h, *, name):
    t, k = x.shape
    n = w.shape[2]
    tk = k if k <= 4096 else _tile(k, 2048)
    nk = k // tk
    assert nk == 1 or nk >= 2
    tm, tn = _tile(t, 1024), _tile(n, 512 if nk == 1 else 1024)
    out_specs, out_shape = _stream_out(t, n, tm, tn)
    return pl.pallas_call(
        functools.partial(_resid_mm_kernel, nk=nk),
        grid=(t // tm, n // tn, nk),
        in_specs=[pl.BlockSpec((tm, tk), lambda i, j, kk: (i, kk)),
                  _wspec(layer, tk, tn),
                  pl.BlockSpec((tm, tn), lambda i, j, kk: (i, j))],
        out_specs=out_specs,
        out_shape=out_shape,
        compiler_params=_params("parallel", "arbitrary", "arbitrary"),
        name=name,
    )(x, w, h)


def _ple_kernel(hb_ref, wg_ref, ss_ref, p_ref, wp_ref, h_ref, o_ref, ob_ref, sso_ref, *, d):
    gate = jax.nn.sigmoid(_dot(hb_ref[...], wg_ref[...]) * _rstd(ss_ref, d))
    emb = _dot(p_ref[...].astype(BF16), wp_ref[...])
    _emit_stream(h_ref[...] + emb * gate, o_ref, ob_ref, sso_ref, pl.program_id(1))


def _ple(hb, ss, wg, p, wp, layer, h):
    t, k = hb.shape
    kp = p.shape[1]
    n = wg.shape[2]
    tm, tn = _tile(t, 1024), _tile(n, 512)
    out_specs, out_shape = _stream_out(t, n, tm, tn)
    return pl.pallas_call(
        functools.partial(_ple_kernel, d=k),
        grid=(t // tm, n // tn),
        in_specs=[pl.BlockSpec((tm, k), lambda i, j: (i, 0)),
                  _wspec(layer, k, tn),
                  pl.BlockSpec((tm, 1), lambda i, j: (i, 0)),
                  pl.BlockSpec((tm, kp), lambda i, j: (i, 0)),
                  _wspec(layer, kp, tn),
                  pl.BlockSpec((tm, tn), lambda i, j: (i, j))],
        out_specs=out_specs,
        out_shape=out_shape,
        compiler_params=_params("parallel", "arbitrary"),
        name="ple",
    )(hb, wg, ss, p, wp, h)


def _attn_kernel(q_ref, k_ref, v_ref, tri_ref, o_ref, acc_ref, carry_ref, *, tq, nheads):
    qi = pl.program_id(2)
    tri = tri_ref[...]
    row = lax.broadcasted_iota(jnp.int32, (tq, tq), 0)
    col = lax.broadcasted_iota(jnp.int32, (tq, tq), 1)
    causal = col < row

    def key_block(kb, masked):
        start = pl.multiple_of(kb * tq, tq)
        heads = range(nheads)
        lanes = [slice(hd * HEAD_DIM, (hd + 1) * HEAD_DIM) for hd in heads]
        zs = [lax.dot_general(q_ref[:, lanes[hd]], k_ref[pl.ds(start, tq), lanes[hd]],
                              (((1,), (1,)), ((), ())), preferred_element_type=F32) for hd in heads]
        tails = []
        for z in zs:
            neg_abs = lax.bitcast_convert_type(
                lax.bitcast_convert_type(z, jnp.uint32) | jnp.uint32(0x80000000), F32)
            softplus = jnp.maximum(z, 0.0) + jnp.log2(1.0 + jnp.exp2(neg_abs))
            if masked:
                softplus = jnp.where(causal, softplus, 0.0)
            hi = softplus.astype(BF16)
            lo = (softplus - hi.astype(F32)).astype(BF16)
            tails.append(_dot(jnp.concatenate([hi, lo], axis=1), tri))
        for hd in heads:
            w = jnp.exp2(zs[hd] + tails[hd] + carry_ref[hd])
            if masked:
                w = jnp.where(causal, w, 0.0)
            acc_ref[:, lanes[hd]] += _dot(w.astype(BF16), v_ref[pl.ds(start, tq), lanes[hd]])
            carry_ref[hd] += tails[hd][:, 0:1]

    acc_ref[...] = jnp.zeros_like(acc_ref)
    carry_ref[...] = jnp.zeros_like(carry_ref)
    key_block(qi, True)

    def body(i, c):
        key_block(qi - 1 - i, False)
        return c

    lax.fori_loop(0, qi, body, 0)
    o_ref[...] = acc_ref[...].astype(o_ref.dtype)


def _attention(proj, *, batch, seq, heads):
    tq = _tile(seq, 256)
    nq = seq // tq
    nheads = _tile(heads, ATTN_HEADS_PER_STEP)
    hgroups = heads // nheads
    width = nheads * HEAD_DIM
    r = jnp.arange(tq)
    tri = -(r[:, None] >= r[None, :]).astype(BF16)
    tri2 = jnp.concatenate([tri, tri], axis=0)
    return pl.pallas_call(
        functools.partial(_attn_kernel, tq=tq, nheads=nheads),
        grid=(batch, hgroups, nq),
        in_specs=[pl.BlockSpec((tq, width), lambda b, h, i: (b * nq + i, h)),
                  pl.BlockSpec((seq, width), lambda b, h, i: (b, hgroups + h)),
                  pl.BlockSpec((seq, width), lambda b, h, i: (b, 2 * hgroups + h)),
                  pl.BlockSpec((2 * tq, tq), lambda b, h, i: (0, 0))],
        out_specs=pl.BlockSpec((tq, width), lambda b, h, i: (b * nq + i, h)),
        out_shape=jax.ShapeDtypeStruct((batch * seq, heads * HEAD_DIM), BF16),
        scratch_shapes=[pltpu.VMEM((tq, width), F32), pltpu.VMEM((nheads, tq, 1), F32)],
        compiler_params=_params("parallel", "parallel", "parallel"),
        name="stickbreak_attn",
    )(proj, proj, proj, tri2)


def _ssm_kernel(u_ref, b_ref, m_ref, c_ref, d_ref, o_ref, x_ref, *, seq):
    ns = SLAB_STATES
    u = u_ref[...]
    x_ref[...] = _dot(u, b_ref[0])

    def step(i, carry):
        c_re, c_im = carry
        r0 = pl.multiple_of(i * SUBLANES, SUBLANES)
        x_re = x_ref[pl.ds(r0, SUBLANES), 0:ns]
        x_im = x_ref[pl.ds(r0, SUBLANES), ns:2 * ns]
        for n, shift in enumerate((1, 2, 4)):
            a_re = m_ref[0, 2 * n]
            a_im = m_ref[0, 2 * n + 1]
            s_re = pltpu.roll(x_re, shift, 0)
            s_im = pltpu.roll(x_im, shift, 0)
            x_re, x_im = (x_re + (a_re * s_re - a_im * s_im),
                          x_im + (a_re * s_im + a_im * s_re))
        p_re = m_ref[0, 6]
        p_im = m_ref[0, 7]
        x_re, x_im = (x_re + (p_re * c_re - p_im * c_im),
                      x_im + (p_re * c_im + p_im * c_re))
        x_ref[pl.ds(r0, SUBLANES), 0:ns] = x_re
        x_ref[pl.ds(r0, SUBLANES), ns:2 * ns] = x_im
        last = SUBLANES - 1
        return (jnp.broadcast_to(x_re[last:last + 1, :], (SUBLANES, ns)),
                jnp.broadcast_to(x_im[last:last + 1, :], (SUBLANES, ns)))

    zero = jnp.zeros((SUBLANES, ns), F32)
    lax.fori_loop(0, seq // SUBLANES, step, (zero, zero))

    y = _dot(x_ref[...].astype(BF16), c_ref[0]) + d_ref[...] * u.astype(F32)
    o_ref[...] = jax.nn.gelu(y, approximate=True).astype(o_ref.dtype)


def _ssm_operands(lam_re, lam_im, log_dt, b_re, b_im, c_re, c_im):
    g = lam_re.shape[0]
    nslab = g // SLAB_GROUPS
    dt = jnp.exp(log_dt)[:, None]
    lam = lax.complex(lam_re, lam_im)
    lam_bar = jnp.exp(lam * dt)
    b_bar = ((lam_bar - 1.0) / lam)[..., None] * lax.complex(b_re, b_im)
    eye = jnp.eye(SLAB_GROUPS, dtype=F32)

    def b_block(part):
        t = part.reshape(nslab, SLAB_GROUPS, SSM_STATE, SSM_GROUP).transpose(0, 1, 3, 2)
        return jnp.einsum("sgcp,gh->sgchp", t, eye).reshape(nslab, SLAB, SLAB_STATES)

    def c_block(part):
        t = part.reshape(nslab, SLAB_GROUPS, SSM_GROUP, SSM_STATE).transpose(0, 1, 3, 2)
        return jnp.einsum("sgpc,gh->sgphc", t, eye).reshape(nslab, SLAB_STATES, SLAB)

    b_mat = jnp.concatenate([b_block(b_bar.real), b_block(b_bar.imag)], axis=2).astype(BF16)
    c_mat = jnp.concatenate([c_block(c_re), c_block(-c_im)], axis=1).astype(BF16)

    lam1 = lam_bar.reshape(nslab, 1, SLAB_STATES)
    lam2 = lam1 * lam1
    lam4 = lam2 * lam2
    rows = jnp.arange(SUBLANES)[None, :, None]
    planes = []
    for lam_pow, shift in ((lam1, 1), (lam2, 2), (lam4, 4)):
        a = jnp.where(rows >= shift, lam_pow, 0.0)
        planes += [a.real, a.imag]
    pows = [lam1]
    for _ in range(SUBLANES - 1):
        pows.append(pows[-1] * lam1)
    p = jnp.concatenate(pows, axis=1)
    planes += [p.real, p.imag]
    mult = jnp.stack(planes, axis=1).astype(F32)
    return b_mat, mult, c_mat


def _ssm(proj, u_col0, b_mat, mult, c_mat, d_skip, *, batch, seq):
    nslab = b_mat.shape[0]
    width = nslab * SLAB
    assert u_col0 % SLAB == 0
    joff = u_col0 // SLAB
    return pl.pallas_call(
        functools.partial(_ssm_kernel, seq=seq),
        grid=(batch, nslab),
        in_specs=[pl.BlockSpec((seq, SLAB), lambda b, s: (b, s + joff)),
                  pl.BlockSpec((1, SLAB, 2 * SLAB_STATES), lambda b, s: (s, 0, 0)),
                  pl.BlockSpec((1, 8, SUBLANES, SLAB_STATES), lambda b, s: (s, 0, 0, 0)),
                  pl.BlockSpec((1, 2 * SLAB_STATES, SLAB), lambda b, s: (s, 0, 0)),
                  pl.BlockSpec((1, SLAB), lambda b, s: (0, s))],
        out_specs=pl.BlockSpec((seq, SLAB), lambda b, s: (b, s)),
        out_shape=jax.ShapeDtypeStruct((batch * seq, width), BF16),
        scratch_shapes=[pltpu.VMEM((seq, 2 * SLAB_STATES), F32)],
        compiler_params=_params("parallel", "parallel"),
        name="s5_ssm",
    )(proj, b_mat, mult, c_mat, d_skip.reshape(1, width))


def kernel(x, p, g_mix, w_in, w_br_attn, lam_re, lam_im, log_dt, b_re, b_im, c_re, c_im, d_skip, w_glu, w_br_ssm, w_o, g_mlp, w_ff1, w_ff2, g_ple, w_ple_gate, w_ple, g_final):
    bsz, seq, d = x.shape
    depth = w_in.shape[0]
    attn_w = w_br_attn.shape[1]
    ssm_w = w_br_ssm.shape[1]
    heads = attn_w // HEAD_DIM
    t = bsz * seq
    mix_w = 3 * attn_w + ssm_w
    assert w_in.shape[2] == mix_w + 2 * d

    wb_in = _to_bf16(w_in, g_mix)
    wb_br_attn = _to_bf16(w_br_attn)
    wb_glu = _to_bf16(w_glu)
    wb_br_ssm = _to_bf16(w_br_ssm)
    wb_o = _to_bf16(w_o)
    wb_ff1 = _to_bf16(w_ff1, g_mlp)
    wb_ff2 = _to_bf16(w_ff2)
    wb_ple_gate = _to_bf16(w_ple_gate, g_ple)
    wb_ple = _to_bf16(w_ple)
    q_scale = jnp.where(jnp.arange(mix_w) < attn_w, ATTN_Q_SCALE, 1.0).astype(F32)

    h = x.reshape(t, d)
    hb, ss = _enter(h)
    for i in range(depth):
        proj = _normed_matmul(hb, ss, wb_in, i, col0=0, ncols=mix_w, act="colscale",
                              col_scale=q_scale, name="proj_mix")
        gates = _normed_matmul(hb, ss, wb_in, i, col0=mix_w, ncols=2 * d, act="sigmoid",
                               name="proj_gates")

        attn = _attention(proj, batch=bsz, seq=seq, heads=heads)
        b_mat, mult, c_mat = _ssm_operands(lam_re[i], lam_im[i], log_dt[i], b_re[i], b_im[i],
                                           c_re[i], c_im[i])
        y = _ssm(proj, 3 * attn_w, b_mat, mult, c_mat, d_skip[i], batch=bsz, seq=seq)
        ssm = _glu(y, wb_glu, i)
        merged = _merge(attn, wb_br_attn, ssm, wb_br_ssm, gates, i)
        h, hb, ss = _resid_matmul(merged, wb_o, i, h, name="out_proj")

        act = _normed_matmul(hb, ss, wb_ff1, i, col0=0, ncols=w_ff1.shape[2], act="relu2", name="ff1")
        h, hb, ss = _resid_matmul(act, wb_ff2, i, h, name="ff2")

        h, hb, ss = _ple(hb, ss, wb_ple_gate, p[i].reshape(t, -1), wb_ple, i, h)
    return _rmsnorm(h, g_final, x.dtype).reshape(bsz, seq, d)
```

```python
import functools
import math
from typing import NamedTuple

import jax
import jax.numpy as jnp
from jax import lax
from jax.experimental import pallas as pl
from jax.experimental.pallas import tpu as pltpu

F32 = jnp.float32
BF16 = jnp.bfloat16

RMS_EPS = 1e-6
HEAD_DIM = 128
ATTN_Q_SCALE = HEAD_DIM ** -0.5 * math.log2(math.e)
ATTN_HEADS_PER_STEP = 8
SSM_GROUP = 16
SSM_STATE = 64
SLAB = 256
SLAB_GROUPS = SLAB // SSM_GROUP
SLAB_STATES = SLAB_GROUPS * SSM_STATE
SUBLANES = 8
SSM_CHUNK = 256
V7X_VMEM_LIMIT = 56 * 1024 * 1024
CAST_BLOCK_BYTES = 8 * 1024 * 1024


def _tile(dim, pref):
    if dim % pref == 0:
        return pref
    assert dim < pref, (dim, pref)
    return dim


def _params(*sem):
    return pltpu.CompilerParams(dimension_semantics=sem, vmem_limit_bytes=V7X_VMEM_LIMIT)


def _dot(a, b):
    return jnp.dot(a, b, preferred_element_type=F32)


def _wspec(w, layer, k, tn, joff=0):
    if w.ndim == 2:
        return pl.BlockSpec((k, tn), lambda i, j, *kk: (kk[0] if kk else 0, j + joff))
    return pl.BlockSpec((None, k, tn), lambda i, j, *kk: (layer, kk[0] if kk else 0, j + joff))


class _SideCast(NamedTuple):
    w: jax.Array
    gain: jax.Array
    layer: int


def _side_cast_specs(side, nsteps, step_of):
    depth, k, n = side.w.shape
    assert k % nsteps == 0 and (k // nsteps) % 16 == 0, (k, nsteps)
    bk = k // nsteps
    in_specs = [pl.BlockSpec((None, bk, n), lambda *g: (side.layer, step_of(*g), 0)),
                pl.BlockSpec((None, bk, 1), lambda *g: (side.layer, step_of(*g), 0))]
    return in_specs, pl.BlockSpec((bk, n), lambda *g: (step_of(*g), 0)), jax.ShapeDtypeStruct((k, n), BF16)


def _cast_block(w_ref, g_ref, o_ref):
    o_ref[...] = (w_ref[...] * g_ref[...]).astype(o_ref.dtype)


def _rstd(ss_ref, d):
    return lax.rsqrt(ss_ref[...] * (1.0 / d) + RMS_EPS)


def _emit_stream(h, h_ref, hb_ref, ss_ref, j):
    h_ref[...] = h
    hb_ref[...] = h.astype(BF16)
    row_ss = jnp.sum(h * h, axis=1, keepdims=True)

    @pl.when(j == 0)
    def _():
        ss_ref[...] = row_ss

    @pl.when(j > 0)
    def _():
        ss_ref[...] += row_ss


def _stream_out(t, n, tm, tn):
    specs = [pl.BlockSpec((tm, tn), lambda i, j, *kk: (i, j)),
             pl.BlockSpec((tm, tn), lambda i, j, *kk: (i, j)),
             pl.BlockSpec((tm, 1), lambda i, j, *kk: (i, 0))]
    shapes = [jax.ShapeDtypeStruct((t, n), F32), jax.ShapeDtypeStruct((t, n), BF16),
              jax.ShapeDtypeStruct((t, 1), F32)]
    return specs, shapes


def _to_bf16(w, gain=None, layer=None):
    depth, k, n = w.shape
    bk = max(16, min(k, CAST_BLOCK_BYTES // (4 * n)))
    assert k % bk == 0, (k, bk)
    if gain is None:
        gain = jnp.ones((depth, k), F32)
    if layer is None:
        grid, first = (depth, k // bk), 0
        out_spec = pl.BlockSpec((None, bk, n), lambda a, r: (a, r, 0))
        out_shape = jax.ShapeDtypeStruct(w.shape, BF16)
    else:
        grid, first = (1, k // bk), layer
        out_spec = pl.BlockSpec((bk, n), lambda a, r: (r, 0))
        out_shape = jax.ShapeDtypeStruct((k, n), BF16)
    return pl.pallas_call(
        _cast_block,
        grid=grid,
        in_specs=[pl.BlockSpec((None, bk, n), lambda a, r: (a + first, r, 0)),
                  pl.BlockSpec((None, bk, 1), lambda a, r: (a + first, r, 0))],
        out_specs=out_spec,
        out_shape=out_shape,
        compiler_params=_params("parallel", "parallel"),
        name="cast_bf16",
    )(w, gain.reshape(depth, k, 1))


def _enter_kernel(x_ref, xb_ref, ss_ref):
    x = x_ref[...]
    xb_ref[...] = x.astype(BF16)
    ss_ref[...] = jnp.sum(x * x, axis=1, keepdims=True)


def _enter(x):
    t, d = x.shape
    tm = _tile(t, 256)
    return pl.pallas_call(
        _enter_kernel,
        grid=(t // tm,),
        in_specs=[pl.BlockSpec((tm, d), lambda i: (i, 0))],
        out_specs=[pl.BlockSpec((tm, d), lambda i: (i, 0)), pl.BlockSpec((tm, 1), lambda i: (i, 0))],
        out_shape=[jax.ShapeDtypeStruct((t, d), BF16), jax.ShapeDtypeStruct((t, 1), F32)],
        compiler_params=_params("parallel"),
        name="enter",
    )(x)


def _rmsnorm_kernel(x_ref, g_ref, o_ref):
    x = x_ref[...]
    ms = jnp.mean(x * x, axis=-1, keepdims=True)
    o_ref[...] = (x * lax.rsqrt(ms + RMS_EPS) * g_ref[...]).astype(o_ref.dtype)


def _rmsnorm(x, g, out_dtype):
    t, d = x.shape
    tm = _tile(t, 256)
    return pl.pallas_call(
        _rmsnorm_kernel,
        grid=(t // tm,),
        in_specs=[pl.BlockSpec((tm, d), lambda i: (i, 0)),
                  pl.BlockSpec((1, d), lambda i: (0, 0))],
        out_specs=pl.BlockSpec((tm, d), lambda i: (i, 0)),
        out_shape=jax.ShapeDtypeStruct((t, d), out_dtype),
        compiler_params=_params("parallel"),
        name="rmsnorm",
    )(x, g.reshape(1, d))


def _normed_mm_kernel(hb_ref, w_ref, ss_ref, s_ref, *rest, act, d, gate_j0, side):
    if side:
        sw_ref, sg_ref, o_ref, so_ref = rest
        _cast_block(sw_ref, sg_ref, so_ref)
    else:
        (o_ref,) = rest
    acc = _dot(hb_ref[...], w_ref[...]) * _rstd(ss_ref, d)
    if act == "relu2":
        acc = jnp.square(jnp.maximum(acc, 0.0))
    elif act == "proj":
        acc = jnp.where(pl.program_id(1) >= gate_j0, jax.nn.sigmoid(acc), acc * s_ref[...])
    o_ref[...] = acc.astype(o_ref.dtype)


def _normed_matmul(hb, ss, w, layer, *, act, name, col_scale=None, gate_col0=None, side=None):
    t, k = hb.shape
    ncols = w.shape[-1]
    tm, tn = _tile(t, 1024), _tile(ncols, 1024)
    nj = ncols // tn
    if act == "proj":
        assert gate_col0 % tn == 0
    else:
        gate_col0, col_scale = 0, jnp.ones((ncols,), F32)
    in_specs = [pl.BlockSpec((tm, k), lambda i, j: (i, 0)),
                _wspec(w, layer, k, tn),
                pl.BlockSpec((tm, 1), lambda i, j: (i, 0)),
                pl.BlockSpec((1, tn), lambda i, j: (0, j))]
    args = [hb, w, ss, col_scale.reshape(1, ncols)]
    out_specs = [pl.BlockSpec((tm, tn), lambda i, j: (i, j))]
    out_shape = [jax.ShapeDtypeStruct((t, ncols), BF16)]
    if side is not None:
        s_in, s_out, s_shape = _side_cast_specs(side, (t // tm) * nj, lambda i, j: i * nj + j)
        in_specs += s_in
        args += [side.w, side.gain.reshape(*side.gain.shape, 1)]
        out_specs.append(s_out)
        out_shape.append(s_shape)
    outs = pl.pallas_call(
        functools.partial(_normed_mm_kernel, act=act, d=k, gate_j0=gate_col0 // tn, side=side is not None),
        grid=(t // tm, nj),
        in_specs=in_specs,
        out_specs=out_specs,
        out_shape=out_shape,
        compiler_params=_params("parallel", "parallel"),
        name=name,
    )(*args)
    return outs if side is not None else outs[0]


def _glu_kernel(y_ref, w_ref, yt_ref, o_ref):
    gate = jax.nn.sigmoid(_dot(y_ref[...], w_ref[...]))
    o_ref[...] = (yt_ref[...].astype(F32) * gate).astype(o_ref.dtype)


def _glu(y, w, layer):
    t, k = y.shape
    n = w.shape[-1]
    tm, tn = _tile(t, 1024), _tile(n, 1024)
    return pl.pallas_call(
        _glu_kernel,
        grid=(t // tm, n // tn),
        in_specs=[pl.BlockSpec((tm, k), lambda i, j: (i, 0)),
                  _wspec(w, layer, k, tn),
                  pl.BlockSpec((tm, tn), lambda i, j: (i, j))],
        out_specs=pl.BlockSpec((tm, tn), lambda i, j: (i, j)),
        out_shape=jax.ShapeDtypeStruct((t, n), BF16),
        compiler_params=_params("parallel", "parallel"),
        name="glu",
    )(y, w, y)


def _merge_kernel(a_ref, wa_ref, s_ref, ws_ref, ga_ref, gs_ref, o_ref):
    up_a = _dot(a_ref[...], wa_ref[...])
    up_s = _dot(s_ref[...], ws_ref[...])
    o_ref[...] = (ga_ref[...].astype(F32) * up_a + gs_ref[...].astype(F32) * up_s).astype(o_ref.dtype)


def _merge(attn, wa, ssm, ws, proj, gate_col0, layer):
    t, ka = attn.shape
    ks = ssm.shape[1]
    d = wa.shape[-1]
    tm, tn = _tile(t, 1024), _tile(d, 1024)
    nj = d // tn
    assert gate_col0 % tn == 0
    goff = gate_col0 // tn
    return pl.pallas_call(
        _merge_kernel,
        grid=(t // tm, nj),
        in_specs=[pl.BlockSpec((tm, ka), lambda i, j: (i, 0)),
                  _wspec(wa, layer, ka, tn),
                  pl.BlockSpec((tm, ks), lambda i, j: (i, 0)),
                  _wspec(ws, layer, ks, tn),
                  pl.BlockSpec((tm, tn), lambda i, j: (i, j + goff)),
                  pl.BlockSpec((tm, tn), lambda i, j: (i, j + goff + nj))],
        out_specs=pl.BlockSpec((tm, tn), lambda i, j: (i, j)),
        out_shape=jax.ShapeDtypeStruct((t, d), BF16),
        compiler_params=_params("parallel", "parallel"),
        name="merge",
    )(attn, wa, ssm, ws, proj, proj)


def _resid_mm_kernel(x_ref, w_ref, h_ref, *rest, nk, side):
    if side:
        sw_ref, sg_ref, o_ref, ob_ref, ss_ref, so_ref = rest
        _cast_block(sw_ref, sg_ref, so_ref)
    else:
        o_ref, ob_ref, ss_ref = rest
    j = pl.program_id(1)
    if nk == 1:
        _emit_stream(h_ref[...] + _dot(x_ref[...], w_ref[...]), o_ref, ob_ref, ss_ref, j)
        return
    k = pl.program_id(2)

    @pl.when(k == 0)
    def _():
        o_ref[...] = h_ref[...] + _dot(x_ref[...], w_ref[...])

    @pl.when(jnp.logical_and(k > 0, k < nk - 1))
    def _():
        o_ref[...] += _dot(x_ref[...], w_ref[...])

    @pl.when(k == nk - 1)
    def _():
        _emit_stream(o_ref[...] + _dot(x_ref[...], w_ref[...]), o_ref, ob_ref, ss_ref, j)


def _resid_matmul(x, w, layer, h, *, name, side=None):
    t, k = x.shape
    n = w.shape[-1]
    tk = k if k <= 4096 else _tile(k, 2048)
    nk = k // tk
    assert nk == 1 or nk >= 2
    tm, tn = _tile(t, 1024), _tile(n, 512 if nk == 1 else 1024)
    nj = n // tn
    in_specs = [pl.BlockSpec((tm, tk), lambda i, j, kk: (i, kk)),
                _wspec(w, layer, tk, tn),
                pl.BlockSpec((tm, tn), lambda i, j, kk: (i, j))]
    args = [x, w, h]
    out_specs, out_shape = _stream_out(t, n, tm, tn)
    if side is not None:
        s_in, s_out, s_shape = _side_cast_specs(side, (t // tm) * nj * nk,
                                                lambda i, j, kk: (i * nj + j) * nk + kk)
        in_specs += s_in
        args += [side.w, side.gain.reshape(*side.gain.shape, 1)]
        out_specs.append(s_out)
        out_shape.append(s_shape)
    return pl.pallas_call(
        functools.partial(_resid_mm_kernel, nk=nk, side=side is not None),
        grid=(t // tm, nj, nk),
        in_specs=in_specs,
        out_specs=out_specs,
        out_shape=out_shape,
        compiler_params=_params("parallel", "arbitrary", "arbitrary"),
        name=name,
    )(*args)


def _ple_kernel(hb_ref, wg_ref, ss_ref, p_ref, wp_ref, h_ref, o_ref, ob_ref, sso_ref, *, d):
    gate = jax.nn.sigmoid(_dot(hb_ref[...], wg_ref[...]) * _rstd(ss_ref, d))
    emb = _dot(p_ref[...].astype(BF16), wp_ref[...])
    _emit_stream(h_ref[...] + emb * gate, o_ref, ob_ref, sso_ref, pl.program_id(1))


def _ple(hb, ss, wg, p, wp, layer, h):
    t, k = hb.shape
    kp = p.shape[1]
    n = wg.shape[-1]
    tm, tn = _tile(t, 1024), _tile(n, 512)
    out_specs, out_shape = _stream_out(t, n, tm, tn)
    return pl.pallas_call(
        functools.partial(_ple_kernel, d=k),
        grid=(t // tm, n // tn),
        in_specs=[pl.BlockSpec((tm, k), lambda i, j: (i, 0)),
                  _wspec(wg, layer, k, tn),
                  pl.BlockSpec((tm, 1), lambda i, j: (i, 0)),
                  pl.BlockSpec((tm, kp), lambda i, j: (i, 0)),
                  _wspec(wp, layer, kp, tn),
                  pl.BlockSpec((tm, tn), lambda i, j: (i, j))],
        out_specs=out_specs,
        out_shape=out_shape,
        compiler_params=_params("parallel", "arbitrary"),
        name="ple",
    )(hb, wg, ss, p, wp, h)


def _attn_kernel(q_ref, k_ref, v_ref, tri_ref, o_ref, acc_ref, carry_ref, *, tq, nheads):
    qi = pl.program_id(2)
    tri = tri_ref[...]
    row = lax.broadcasted_iota(jnp.int32, (tq, tq), 0)
    col = lax.broadcasted_iota(jnp.int32, (tq, tq), 1)
    causal = col < row

    def key_block(kb, masked):
        start = pl.multiple_of(kb * tq, tq)
        heads = range(nheads)
        lanes = [slice(hd * HEAD_DIM, (hd + 1) * HEAD_DIM) for hd in heads]
        zs = [lax.dot_general(q_ref[:, lanes[hd]], k_ref[pl.ds(start, tq), lanes[hd]],
                              (((1,), (1,)), ((), ())), preferred_element_type=F32) for hd in heads]
        tails = []
        for z in zs:
            neg_abs = lax.bitcast_convert_type(
                lax.bitcast_convert_type(z, jnp.uint32) | jnp.uint32(0x80000000), F32)
            softplus = jnp.maximum(z, 0.0) + jnp.log2(1.0 + jnp.exp2(neg_abs))
            if masked:
                softplus = jnp.where(causal, softplus, 0.0)
            hi = softplus.astype(BF16)
            lo = (softplus - hi.astype(F32)).astype(BF16)
            tails.append(_dot(jnp.concatenate([hi, lo], axis=1), tri))
        for hd in heads:
            w = jnp.exp2(zs[hd] + tails[hd] + carry_ref[hd])
            if masked:
                w = jnp.where(causal, w, 0.0)
            acc_ref[:, lanes[hd]] += _dot(w.astype(BF16), v_ref[pl.ds(start, tq), lanes[hd]])
            carry_ref[hd] += tails[hd][:, 0:1]

    acc_ref[...] = jnp.zeros_like(acc_ref)
    carry_ref[...] = jnp.zeros_like(carry_ref)
    key_block(qi, True)

    def body(i, c):
        key_block(qi - 1 - i, False)
        return c

    lax.fori_loop(0, qi, body, 0)
    o_ref[...] = acc_ref[...].astype(o_ref.dtype)


def _attention(proj, *, batch, seq, heads):
    tq = _tile(seq, 256)
    nq = seq // tq
    nheads = _tile(heads, ATTN_HEADS_PER_STEP)
    hgroups = heads // nheads
    width = nheads * HEAD_DIM
    r = jnp.arange(tq)
    tri = -(r[:, None] >= r[None, :]).astype(BF16)
    tri2 = jnp.concatenate([tri, tri], axis=0)
    return pl.pallas_call(
        functools.partial(_attn_kernel, tq=tq, nheads=nheads),
        grid=(batch, hgroups, nq),
        in_specs=[pl.BlockSpec((tq, width), lambda b, h, i: (b * nq + i, h)),
                  pl.BlockSpec((seq, width), lambda b, h, i: (b, hgroups + h)),
                  pl.BlockSpec((seq, width), lambda b, h, i: (b, 2 * hgroups + h)),
                  pl.BlockSpec((2 * tq, tq), lambda b, h, i: (0, 0))],
        out_specs=pl.BlockSpec((tq, width), lambda b, h, i: (b * nq + i, h)),
        out_shape=jax.ShapeDtypeStruct((batch * seq, heads * HEAD_DIM), BF16),
        scratch_shapes=[pltpu.VMEM((tq, width), F32), pltpu.VMEM((nheads, tq, 1), F32)],
        compiler_params=_params("parallel", "parallel", "parallel"),
        name="stickbreak_attn",
    )(proj, proj, proj, tri2)


def _ssm_kernel(u_ref, b_ref, m_ref, c_ref, d_ref, o_ref, x_ref, *, seq):
    ns = SLAB_STATES
    nchunks = seq // SSM_CHUNK

    def rows(k):
        return slice(k * SSM_CHUNK, (k + 1) * SSM_CHUNK)

    def project(k):
        x_ref[rows(k), :] = _dot(u_ref[rows(k), :], b_ref[0])

    def scan(k, carry):
        c_re, c_im = carry
        for s in range(SSM_CHUNK // SUBLANES):
            r = slice(k * SSM_CHUNK + s * SUBLANES, k * SSM_CHUNK + (s + 1) * SUBLANES)
            x_re = x_ref[r, 0:ns]
            x_im = x_ref[r, ns:2 * ns]
            for n, shift in enumerate((1, 2, 4)):
                a_re = m_ref[0, 2 * n]
                a_im = m_ref[0, 2 * n + 1]
                s_re = pltpu.roll(x_re, shift, 0)
                s_im = pltpu.roll(x_im, shift, 0)
                x_re, x_im = (x_re + (a_re * s_re - a_im * s_im),
                              x_im + (a_re * s_im + a_im * s_re))
            p_re = m_ref[0, 6]
            p_im = m_ref[0, 7]
            x_re, x_im = (x_re + (p_re * c_re - p_im * c_im),
                          x_im + (p_re * c_im + p_im * c_re))
            x_ref[r, 0:ns] = x_re
            x_ref[r, ns:2 * ns] = x_im
            last = SUBLANES - 1
            c_re = jnp.broadcast_to(x_re[last:last + 1, :], (SUBLANES, ns))
            c_im = jnp.broadcast_to(x_im[last:last + 1, :], (SUBLANES, ns))
        return c_re, c_im

    def readout(k):
        y = (_dot(x_ref[rows(k), :].astype(BF16), c_ref[0])
             + d_ref[...] * u_ref[rows(k), :].astype(F32))
        o_ref[rows(k), :] = jax.nn.gelu(y, approximate=True).astype(o_ref.dtype)

    project(0)
    if nchunks > 1:
        project(1)
    zero = jnp.zeros((SUBLANES, ns), F32)
    carry = scan(0, (zero, zero))
    for k in range(1, nchunks):
        readout(k - 1)
        if k + 1 < nchunks:
            project(k + 1)
        carry = scan(k, carry)
    readout(nchunks - 1)


def _ssm_operands(lam_re, lam_im, log_dt, b_re, b_im, c_re, c_im):
    g = lam_re.shape[0]
    nslab = g // SLAB_GROUPS
    dt = jnp.exp(log_dt)[:, None]
    lam = lax.complex(lam_re, lam_im)
    lam_bar = jnp.exp(lam * dt)
    b_bar = ((lam_bar - 1.0) / lam)[..., None] * lax.complex(b_re, b_im)
    eye = jnp.eye(SLAB_GROUPS, dtype=F32)

    def b_block(part):
        t = part.reshape(nslab, SLAB_GROUPS, SSM_STATE, SSM_GROUP).transpose(0, 1, 3, 2)
        return jnp.einsum("sgcp,gh->sgchp", t, eye).reshape(nslab, SLAB, SLAB_STATES)

    def c_block(part):
        t = part.reshape(nslab, SLAB_GROUPS, SSM_GROUP, SSM_STATE).transpose(0, 1, 3, 2)
        return jnp.einsum("sgpc,gh->sgphc", t, eye).reshape(nslab, SLAB_STATES, SLAB)

    b_mat = jnp.concatenate([b_block(b_bar.real), b_block(b_bar.imag)], axis=2).astype(BF16)
    c_mat = jnp.concatenate([c_block(c_re), c_block(-c_im)], axis=1).astype(BF16)

    lam1 = lam_bar.reshape(nslab, 1, SLAB_STATES)
    lam2 = lam1 * lam1
    lam4 = lam2 * lam2
    rows = jnp.arange(SUBLANES)[None, :, None]
    planes = []
    for lam_pow, shift in ((lam1, 1), (lam2, 2), (lam4, 4)):
        a = jnp.where(rows >= shift, lam_pow, 0.0)
        planes += [a.real, a.imag]
    pows = [lam1]
    for _ in range(SUBLANES - 1):
        pows.append(pows[-1] * lam1)
    p = jnp.concatenate(pows, axis=1)
    planes += [p.real, p.imag]
    mult = jnp.stack(planes, axis=1).astype(F32)
    return b_mat, mult, c_mat


def _ssm(proj, u_col0, b_mat, mult, c_mat, d_skip, *, batch, seq):
    nslab = b_mat.shape[0]
    width = nslab * SLAB
    assert u_col0 % SLAB == 0
    joff = u_col0 // SLAB
    return pl.pallas_call(
        functools.partial(_ssm_kernel, seq=seq),
        grid=(batch, nslab),
        in_specs=[pl.BlockSpec((seq, SLAB), lambda b, s: (b, s + joff)),
                  pl.BlockSpec((1, SLAB, 2 * SLAB_STATES), lambda b, s: (s, 0, 0)),
                  pl.BlockSpec((1, 8, SUBLANES, SLAB_STATES), lambda b, s: (s, 0, 0, 0)),
                  pl.BlockSpec((1, 2 * SLAB_STATES, SLAB), lambda b, s: (s, 0, 0)),
                  pl.BlockSpec((1, SLAB), lambda b, s: (0, s))],
        out_specs=pl.BlockSpec((seq, SLAB), lambda b, s: (b, s)),
        out_shape=jax.ShapeDtypeStruct((batch * seq, width), BF16),
        scratch_shapes=[pltpu.VMEM((seq, 2 * SLAB_STATES), F32)],
        compiler_params=_params("parallel", "parallel"),
        name="s5_ssm",
    )(proj, b_mat, mult, c_mat, d_skip.reshape(1, width))


def kernel(x, p, g_mix, w_in, w_br_attn, lam_re, lam_im, log_dt, b_re, b_im, c_re, c_im, d_skip, w_glu, w_br_ssm, w_o, g_mlp, w_ff1, w_ff2, g_ple, w_ple_gate, w_ple, g_final):
    bsz, seq, d = x.shape
    depth = w_in.shape[0]
    attn_w = w_br_attn.shape[1]
    ssm_w = w_br_ssm.shape[1]
    heads = attn_w // HEAD_DIM
    t = bsz * seq
    mix_w = 3 * attn_w + ssm_w
    assert w_in.shape[2] == mix_w + 2 * d

    wb_in = _to_bf16(w_in, g_mix, layer=0)
    wb_br_attn = _to_bf16(w_br_attn)
    wb_glu = _to_bf16(w_glu)
    wb_br_ssm = _to_bf16(w_br_ssm)
    wb_o = _to_bf16(w_o)
    wb_ple_gate = _to_bf16(w_ple_gate, g_ple)
    wb_ple = _to_bf16(w_ple)
    q_scale = jnp.where(jnp.arange(mix_w + 2 * d) < attn_w, ATTN_Q_SCALE, 1.0).astype(F32)
    no_gain = jnp.ones(w_ff2.shape[:2], F32)

    h = x.reshape(t, d)
    hb, ss = _enter(h)
    for i in range(depth):
        proj, wb_ff1 = _normed_matmul(hb, ss, wb_in, i, act="proj", col_scale=q_scale, gate_col0=mix_w,
                                      name="proj", side=_SideCast(w_ff1, g_mlp, i))
        attn = _attention(proj, batch=bsz, seq=seq, heads=heads)
        b_mat, mult, c_mat = _ssm_operands(lam_re[i], lam_im[i], log_dt[i], b_re[i], b_im[i],
                                           c_re[i], c_im[i])
        y = _ssm(proj, 3 * attn_w, b_mat, mult, c_mat, d_skip[i], batch=bsz, seq=seq)
        ssm = _glu(y, wb_glu, i)
        merged = _merge(attn, wb_br_attn, ssm, wb_br_ssm, proj, mix_w, i)
        h, hb, ss = _resid_matmul(merged, wb_o, i, h, name="out_proj")

        act, wb_ff2 = _normed_matmul(hb, ss, wb_ff1, i, act="relu2", name="ff1",
                                     side=_SideCast(w_ff2, no_gain, i))
        if i + 1 < depth:
            h, hb, ss, wb_in = _resid_matmul(act, wb_ff2, i, h, name="ff2",
                                             side=_SideCast(w_in, g_mix, i + 1))
        else:
            h, hb, ss = _resid_matmul(act, wb_ff2, i, h, name="ff2")

        h, hb, ss = _ple(hb, ss, wb_ple_gate, p[i].reshape(t, -1), wb_ple, i, h)
    return _rmsnorm(h, g_final, x.dtype).reshape(bsz, seq, d)
```

```python
import functools
import math
from typing import NamedTuple

import jax
import jax.numpy as jnp
from jax import lax
from jax.experimental import pallas as pl
from jax.experimental.pallas import tpu as pltpu

F32 = jnp.float32
BF16 = jnp.bfloat16

RMS_EPS = 1e-6
HEAD_DIM = 128
ATTN_Q_SCALE = HEAD_DIM ** -0.5 * math.log2(math.e)
ATTN_HEADS_PER_STEP = 8
SSM_GROUP = 16
SSM_STATE = 64
SLAB = 256
SLAB_GROUPS = SLAB // SSM_GROUP
SLAB_STATES = SLAB_GROUPS * SSM_STATE
SUBLANES = 8
SSM_CHUNK = 256
V7X_VMEM_LIMIT = 56 * 1024 * 1024
CAST_BLOCK_BYTES = 8 * 1024 * 1024


def _tile(dim, pref):
    if dim % pref == 0:
        return pref
    assert dim < pref, (dim, pref)
    return dim


def _params(*sem):
    return pltpu.CompilerParams(dimension_semantics=sem, vmem_limit_bytes=V7X_VMEM_LIMIT)


def _dot(a, b):
    return jnp.dot(a, b, preferred_element_type=F32)


def _wspec(w, layer, k, tn, joff=0):
    if w.ndim == 2:
        return pl.BlockSpec((k, tn), lambda i, j, *kk: (kk[0] if kk else 0, j + joff))
    return pl.BlockSpec((None, k, tn), lambda i, j, *kk: (layer, kk[0] if kk else 0, j + joff))


class _SideCast(NamedTuple):
    w: jax.Array
    gain: jax.Array
    layer: int


def _side_cast_specs(side, nsteps, step_of):
    depth, k, n = side.w.shape
    assert k % nsteps == 0 and (k // nsteps) % 16 == 0, (k, nsteps)
    bk = k // nsteps
    in_specs = [pl.BlockSpec((None, bk, n), lambda *g: (side.layer, step_of(*g), 0)),
                pl.BlockSpec((None, bk, 1), lambda *g: (side.layer, step_of(*g), 0))]
    return in_specs, pl.BlockSpec((bk, n), lambda *g: (step_of(*g), 0)), jax.ShapeDtypeStruct((k, n), BF16)


def _cast_block(w_ref, g_ref, o_ref):
    o_ref[...] = (w_ref[...] * g_ref[...]).astype(o_ref.dtype)


def _rstd(ss_ref, d):
    return lax.rsqrt(ss_ref[...] * (1.0 / d) + RMS_EPS)


def _emit_stream(h, h_ref, hb_ref, ss_ref, j):
    h_ref[...] = h
    hb_ref[...] = h.astype(BF16)
    row_ss = jnp.sum(h * h, axis=1, keepdims=True)

    @pl.when(j == 0)
    def _():
        ss_ref[...] = row_ss

    @pl.when(j > 0)
    def _():
        ss_ref[...] += row_ss


def _stream_out(t, n, tm, tn):
    specs = [pl.BlockSpec((tm, tn), lambda i, j, *kk: (i, j)),
             pl.BlockSpec((tm, tn), lambda i, j, *kk: (i, j)),
             pl.BlockSpec((tm, 1), lambda i, j, *kk: (i, 0))]
    shapes = [jax.ShapeDtypeStruct((t, n), F32), jax.ShapeDtypeStruct((t, n), BF16),
              jax.ShapeDtypeStruct((t, 1), F32)]
    return specs, shapes


def _to_bf16(w, gain=None, layer=None):
    depth, k, n = w.shape
    bk = max(16, min(k, CAST_BLOCK_BYTES // (4 * n)))
    assert k % bk == 0, (k, bk)
    if gain is None:
        gain = jnp.ones((depth, k), F32)
    if layer is None:
        grid, first = (depth, k // bk), 0
        out_spec = pl.BlockSpec((None, bk, n), lambda a, r: (a, r, 0))
        out_shape = jax.ShapeDtypeStruct(w.shape, BF16)
    else:
        grid, first = (1, k // bk), layer
        out_spec = pl.BlockSpec((bk, n), lambda a, r: (r, 0))
        out_shape = jax.ShapeDtypeStruct((k, n), BF16)
    return pl.pallas_call(
        _cast_block,
        grid=grid,
        in_specs=[pl.BlockSpec((None, bk, n), lambda a, r: (a + first, r, 0)),
                  pl.BlockSpec((None, bk, 1), lambda a, r: (a + first, r, 0))],
        out_specs=out_spec,
        out_shape=out_shape,
        compiler_params=_params("parallel", "parallel"),
        name="cast_bf16",
    )(w, gain.reshape(depth, k, 1))


def _enter_kernel(x_ref, xb_ref, ss_ref):
    x = x_ref[...]
    xb_ref[...] = x.astype(BF16)
    ss_ref[...] = jnp.sum(x * x, axis=1, keepdims=True)


def _enter(x):
    t, d = x.shape
    tm = _tile(t, 256)
    return pl.pallas_call(
        _enter_kernel,
        grid=(t // tm,),
        in_specs=[pl.BlockSpec((tm, d), lambda i: (i, 0))],
        out_specs=[pl.BlockSpec((tm, d), lambda i: (i, 0)), pl.BlockSpec((tm, 1), lambda i: (i, 0))],
        out_shape=[jax.ShapeDtypeStruct((t, d), BF16), jax.ShapeDtypeStruct((t, 1), F32)],
        compiler_params=_params("parallel"),
        name="enter",
    )(x)


def _rmsnorm_kernel(x_ref, g_ref, o_ref):
    x = x_ref[...]
    ms = jnp.mean(x * x, axis=-1, keepdims=True)
    o_ref[...] = (x * lax.rsqrt(ms + RMS_EPS) * g_ref[...]).astype(o_ref.dtype)


def _rmsnorm(x, g, out_dtype):
    t, d = x.shape
    tm = _tile(t, 256)
    return pl.pallas_call(
        _rmsnorm_kernel,
        grid=(t // tm,),
        in_specs=[pl.BlockSpec((tm, d), lambda i: (i, 0)),
                  pl.BlockSpec((1, d), lambda i: (0, 0))],
        out_specs=pl.BlockSpec((tm, d), lambda i: (i, 0)),
        out_shape=jax.ShapeDtypeStruct((t, d), out_dtype),
        compiler_params=_params("parallel"),
        name="rmsnorm",
    )(x, g.reshape(1, d))


def _normed_mm_kernel(hb_ref, w_ref, ss_ref, *rest, d, side):
    if side:
        sw_ref, sg_ref, o_ref, so_ref = rest
        _cast_block(sw_ref, sg_ref, so_ref)
    else:
        (o_ref,) = rest
    acc = _dot(hb_ref[...], w_ref[...]) * _rstd(ss_ref, d)
    o_ref[...] = jnp.square(jnp.maximum(acc, 0.0)).astype(o_ref.dtype)


def _normed_relu2_matmul(hb, ss, w, layer, *, name, side=None):
    t, k = hb.shape
    ncols = w.shape[-1]
    tm, tn = _tile(t, 1024), _tile(ncols, 1024)
    nj = ncols // tn
    in_specs = [pl.BlockSpec((tm, k), lambda i, j: (i, 0)),
                _wspec(w, layer, k, tn),
                pl.BlockSpec((tm, 1), lambda i, j: (i, 0))]
    args = [hb, w, ss]
    out_specs = [pl.BlockSpec((tm, tn), lambda i, j: (i, j))]
    out_shape = [jax.ShapeDtypeStruct((t, ncols), BF16)]
    if side is not None:
        s_in, s_out, s_shape = _side_cast_specs(side, (t // tm) * nj, lambda i, j: i * nj + j)
        in_specs += s_in
        args += [side.w, side.gain.reshape(*side.gain.shape, 1)]
        out_specs.append(s_out)
        out_shape.append(s_shape)
    outs = pl.pallas_call(
        functools.partial(_normed_mm_kernel, d=k, side=side is not None),
        grid=(t // tm, nj),
        in_specs=in_specs,
        out_specs=out_specs,
        out_shape=out_shape,
        compiler_params=_params("parallel", "parallel"),
        name=name,
    )(*args)
    return outs if side is not None else outs[0]


def _proj_kernel(hb_ref, w_ref, ss_ref, qkv_ref, gates_ref, *, d, nq, nqkv):
    j = pl.program_id(1)

    @pl.when(j < nqkv)
    def _():
        scale = jnp.where(j < nq, ATTN_Q_SCALE, 1.0)
        qkv_ref[...] = (_dot(hb_ref[...], w_ref[...]) * (_rstd(ss_ref, d) * scale)).astype(qkv_ref.dtype)

    @pl.when(j >= nqkv)
    def _():
        gates_ref[...] = jax.nn.sigmoid(_dot(hb_ref[...], w_ref[...]) * _rstd(ss_ref, d)).astype(gates_ref.dtype)


def _proj(hb, ss, w, layer, *, attn_w, ssm_w, d_model):
    t, k = hb.shape
    tm = _tile(t, 1024)
    tn = next(c for c in (1024, 512, 256, 128)
              if attn_w % c == 0 and ssm_w % c == 0 and (2 * d_model) % c == 0)
    nq, nqkv, nu, ng = attn_w // tn, 3 * attn_w // tn, ssm_w // tn, 2 * d_model // tn

    def wcol(j):
        return jnp.where(j < nqkv, j, j + nu)

    if w.ndim == 2:
        w_spec = pl.BlockSpec((k, tn), lambda i, j: (0, wcol(j)))
    else:
        w_spec = pl.BlockSpec((None, k, tn), lambda i, j: (layer, 0, wcol(j)))
    return pl.pallas_call(
        functools.partial(_proj_kernel, d=k, nq=nq, nqkv=nqkv),
        grid=(t // tm, nqkv + ng),
        in_specs=[pl.BlockSpec((tm, k), lambda i, j: (i, 0)),
                  w_spec,
                  pl.BlockSpec((tm, 1), lambda i, j: (i, 0))],
        out_specs=[pl.BlockSpec((tm, tn), lambda i, j: (i, jnp.minimum(j, nqkv - 1))),
                   pl.BlockSpec((tm, tn), lambda i, j: (i, jnp.maximum(j - nqkv, 0)))],
        out_shape=[jax.ShapeDtypeStruct((t, 3 * attn_w), BF16),
                   jax.ShapeDtypeStruct((t, 2 * d_model), BF16)],
        compiler_params=_params("parallel", "arbitrary"),
        name="proj",
    )(hb, w, ss)


def _proj_u_kernel(hb_ref, w_ref, ss_ref, o_ref, *, d, nr, sub_len, width):
    acc = _dot(hb_ref[...], w_ref[...]) * _rstd(ss_ref, d)
    for rl in range(nr):
        o_ref[:, rl * width:(rl + 1) * width] = acc[rl * sub_len:(rl + 1) * sub_len, :].astype(o_ref.dtype)


def _proj_u(hb, ss, w, layer, *, col0, width, batch, seq):
    t, k = hb.shape
    sub_len = seq // SUBLANES
    tm = _tile(seq, 512)
    assert tm % sub_len == 0 and col0 % width == 0
    nr = tm // sub_len
    tiles = seq // tm
    if w.ndim == 2:
        w_spec = pl.BlockSpec((k, width), lambda i: (0, col0 // width))
    else:
        w_spec = pl.BlockSpec((None, k, width), lambda i: (layer, 0, col0 // width))
    out = pl.pallas_call(
        functools.partial(_proj_u_kernel, d=k, nr=nr, sub_len=sub_len, width=width),
        grid=(t // tm,),
        in_specs=[pl.BlockSpec((tm, k), lambda i: (i, 0)),
                  w_spec,
                  pl.BlockSpec((tm, 1), lambda i: (i, 0))],
        out_specs=pl.BlockSpec((None, sub_len, nr * width), lambda i: (i // tiles, 0, i % tiles)),
        out_shape=jax.ShapeDtypeStruct((batch, sub_len, SUBLANES * width), BF16),
        compiler_params=_params("parallel"),
        name="proj_u",
    )(hb, w, ss)
    return out.reshape(t, width)


def _glu_kernel(y_ref, w_ref, yt_ref, o_ref):
    gate = jax.nn.sigmoid(_dot(y_ref[...], w_ref[...]))
    o_ref[...] = (yt_ref[...].astype(F32) * gate).astype(o_ref.dtype)


def _glu(y, w, layer):
    t, k = y.shape
    n = w.shape[-1]
    tm, tn = _tile(t, 1024), _tile(n, 1024)
    return pl.pallas_call(
        _glu_kernel,
        grid=(t // tm, n // tn),
        in_specs=[pl.BlockSpec((tm, k), lambda i, j: (i, 0)),
                  _wspec(w, layer, k, tn),
                  pl.BlockSpec((tm, tn), lambda i, j: (i, j))],
        out_specs=pl.BlockSpec((tm, tn), lambda i, j: (i, j)),
        out_shape=jax.ShapeDtypeStruct((t, n), BF16),
        compiler_params=_params("parallel", "parallel"),
        name="glu",
    )(y, w, y)


def _merge_kernel(a_ref, wa_ref, s_ref, ws_ref, ga_ref, gs_ref, o_ref, *, nr, sub_len, width):
    up_a = _dot(a_ref[...], wa_ref[...])
    for rl in range(nr):
        rows = slice(rl * sub_len, (rl + 1) * sub_len)
        up_s = _dot(s_ref[:, rl * width:(rl + 1) * width], ws_ref[...])
        o_ref[rows, :] = (ga_ref[rows, :].astype(F32) * up_a[rows, :]
                          + gs_ref[rows, :].astype(F32) * up_s).astype(o_ref.dtype)


def _merge(attn, wa, ssm_perm, ws, gates, layer, *, batch, seq):
    t, ka = attn.shape
    ks = ssm_perm.shape[1]
    d = wa.shape[-1]
    sub_len = seq // SUBLANES
    tm, tn = _tile(seq, 1024), _tile(d, 1024)
    assert tm % sub_len == 0
    nr = tm // sub_len
    tiles = seq // tm
    nj = d // tn
    return pl.pallas_call(
        functools.partial(_merge_kernel, nr=nr, sub_len=sub_len, width=ks),
        grid=(t // tm, nj),
        in_specs=[pl.BlockSpec((tm, ka), lambda i, j: (i, 0)),
                  _wspec(wa, layer, ka, tn),
                  pl.BlockSpec((None, sub_len, nr * ks), lambda i, j: (i // tiles, 0, i % tiles)),
                  _wspec(ws, layer, ks, tn),
                  pl.BlockSpec((tm, tn), lambda i, j: (i, j)),
                  pl.BlockSpec((tm, tn), lambda i, j: (i, j + nj))],
        out_specs=pl.BlockSpec((tm, tn), lambda i, j: (i, j)),
        out_shape=jax.ShapeDtypeStruct((t, d), BF16),
        compiler_params=_params("parallel", "parallel"),
        name="merge",
    )(attn, wa, ssm_perm.reshape(batch, sub_len, SUBLANES * ks), ws, gates, gates)


def _resid_mm_kernel(x_ref, w_ref, h_ref, *rest, nk, side):
    if side:
        sw_ref, sg_ref, o_ref, ob_ref, ss_ref, so_ref = rest
        _cast_block(sw_ref, sg_ref, so_ref)
    else:
        o_ref, ob_ref, ss_ref = rest
    j = pl.program_id(1)
    if nk == 1:
        _emit_stream(h_ref[...] + _dot(x_ref[...], w_ref[...]), o_ref, ob_ref, ss_ref, j)
        return
    k = pl.program_id(2)

    @pl.when(k == 0)
    def _():
        o_ref[...] = h_ref[...] + _dot(x_ref[...], w_ref[...])

    @pl.when(jnp.logical_and(k > 0, k < nk - 1))
    def _():
        o_ref[...] += _dot(x_ref[...], w_ref[...])

    @pl.when(k == nk - 1)
    def _():
        _emit_stream(o_ref[...] + _dot(x_ref[...], w_ref[...]), o_ref, ob_ref, ss_ref, j)


def _resid_matmul(x, w, layer, h, *, name, side=None):
    t, k = x.shape
    n = w.shape[-1]
    tk = k if k <= 4096 else _tile(k, 2048)
    nk = k // tk
    assert nk == 1 or nk >= 2
    tm, tn = _tile(t, 1024), _tile(n, 512 if nk == 1 else 1024)
    nj = n // tn
    in_specs = [pl.BlockSpec((tm, tk), lambda i, j, kk: (i, kk)),
                _wspec(w, layer, tk, tn),
                pl.BlockSpec((tm, tn), lambda i, j, kk: (i, j))]
    args = [x, w, h]
    out_specs, out_shape = _stream_out(t, n, tm, tn)
    if side is not None:
        s_in, s_out, s_shape = _side_cast_specs(side, (t // tm) * nj * nk,
                                                lambda i, j, kk: (i * nj + j) * nk + kk)
        in_specs += s_in
        args += [side.w, side.gain.reshape(*side.gain.shape, 1)]
        out_specs.append(s_out)
        out_shape.append(s_shape)
    return pl.pallas_call(
        functools.partial(_resid_mm_kernel, nk=nk, side=side is not None),
        grid=(t // tm, nj, nk),
        in_specs=in_specs,
        out_specs=out_specs,
        out_shape=out_shape,
        compiler_params=_params("parallel", "arbitrary", "arbitrary"),
        name=name,
    )(*args)


def _ple_kernel(hb_ref, wg_ref, ss_ref, p_ref, wp_ref, h_ref, o_ref, ob_ref, sso_ref, *, d):
    gate = jax.nn.sigmoid(_dot(hb_ref[...], wg_ref[...]) * _rstd(ss_ref, d))
    emb = _dot(p_ref[...].astype(BF16), wp_ref[...])
    _emit_stream(h_ref[...] + emb * gate, o_ref, ob_ref, sso_ref, pl.program_id(1))


def _ple(hb, ss, wg, p, wp, layer, h):
    t, k = hb.shape
    kp = p.shape[1]
    n = wg.shape[-1]
    tm, tn = _tile(t, 1024), _tile(n, 512)
    out_specs, out_shape = _stream_out(t, n, tm, tn)
    return pl.pallas_call(
        functools.partial(_ple_kernel, d=k),
        grid=(t // tm, n // tn),
        in_specs=[pl.BlockSpec((tm, k), lambda i, j: (i, 0)),
                  _wspec(wg, layer, k, tn),
                  pl.BlockSpec((tm, 1), lambda i, j: (i, 0)),
                  pl.BlockSpec((tm, kp), lambda i, j: (i, 0)),
                  _wspec(wp, layer, kp, tn),
                  pl.BlockSpec((tm, tn), lambda i, j: (i, j))],
        out_specs=out_specs,
        out_shape=out_shape,
        compiler_params=_params("parallel", "arbitrary"),
        name="ple",
    )(hb, wg, ss, p, wp, h)


def _attn_kernel(q_ref, k_ref, v_ref, tri_ref, o_ref, acc_ref, carry_ref, *, tq, nheads):
    qi = pl.program_id(2)
    tri = tri_ref[...]
    row = lax.broadcasted_iota(jnp.int32, (tq, tq), 0)
    col = lax.broadcasted_iota(jnp.int32, (tq, tq), 1)
    causal = col < row

    def key_block(kb, masked):
        start = pl.multiple_of(kb * tq, tq)
        heads = range(nheads)
        lanes = [slice(hd * HEAD_DIM, (hd + 1) * HEAD_DIM) for hd in heads]
        zs = [lax.dot_general(q_ref[:, lanes[hd]], k_ref[pl.ds(start, tq), lanes[hd]],
                              (((1,), (1,)), ((), ())), preferred_element_type=F32) for hd in heads]
        tails = []
        for z in zs:
            neg_abs = lax.bitcast_convert_type(
                lax.bitcast_convert_type(z, jnp.uint32) | jnp.uint32(0x80000000), F32)
            softplus = jnp.maximum(z, 0.0) + jnp.log2(1.0 + jnp.exp2(neg_abs))
            if masked:
                softplus = jnp.where(causal, softplus, 0.0)
            hi = softplus.astype(BF16)
            lo = (softplus - hi.astype(F32)).astype(BF16)
            tails.append(_dot(jnp.concatenate([hi, lo], axis=1), tri))
        for hd in heads:
            w = jnp.exp2(zs[hd] + tails[hd] + carry_ref[hd])
            if masked:
                w = jnp.where(causal, w, 0.0)
            acc_ref[:, lanes[hd]] += _dot(w.astype(BF16), v_ref[pl.ds(start, tq), lanes[hd]])
            carry_ref[hd] += tails[hd][:, 0:1]

    acc_ref[...] = jnp.zeros_like(acc_ref)
    carry_ref[...] = jnp.zeros_like(carry_ref)
    key_block(qi, True)

    def body(i, c):
        key_block(qi - 1 - i, False)
        return c

    lax.fori_loop(0, qi, body, 0)
    o_ref[...] = acc_ref[...].astype(o_ref.dtype)


def _attention(qkv, *, batch, seq, heads):
    tq = _tile(seq, 256)
    nq = seq // tq
    nheads = _tile(heads, ATTN_HEADS_PER_STEP)
    hgroups = heads // nheads
    width = nheads * HEAD_DIM
    r = jnp.arange(tq)
    tri = -(r[:, None] >= r[None, :]).astype(BF16)
    tri2 = jnp.concatenate([tri, tri], axis=0)
    return pl.pallas_call(
        functools.partial(_attn_kernel, tq=tq, nheads=nheads),
        grid=(batch, hgroups, nq),
        in_specs=[pl.BlockSpec((tq, width), lambda b, h, i: (b * nq + i, h)),
                  pl.BlockSpec((seq, width), lambda b, h, i: (b, hgroups + h)),
                  pl.BlockSpec((seq, width), lambda b, h, i: (b, 2 * hgroups + h)),
                  pl.BlockSpec((2 * tq, tq), lambda b, h, i: (0, 0))],
        out_specs=pl.BlockSpec((tq, width), lambda b, h, i: (b * nq + i, h)),
        out_shape=jax.ShapeDtypeStruct((batch * seq, heads * HEAD_DIM), BF16),
        scratch_shapes=[pltpu.VMEM((tq, width), F32), pltpu.VMEM((nheads, tq, 1), F32)],
        compiler_params=_params("parallel", "parallel", "parallel"),
        name="stickbreak_attn",
    )(qkv, qkv, qkv, tri2)


def _ssm_kernel(u_ref, b_ref, m_ref, c_ref, d_ref, o_ref, x_ref, xb_ref, *, seq):
    ns = SLAB_STATES
    nchunks = seq // SSM_CHUNK
    steps = SSM_CHUNK // SUBLANES

    def rows(k):
        return slice(k * SSM_CHUNK, (k + 1) * SSM_CHUNK)

    def group(i):
        return slice(i * SUBLANES, (i + 1) * SUBLANES)

    def project(k):
        x_ref[rows(k), :] = _dot(u_ref[rows(k), :], b_ref[0])

    def scan(k, v):
        v_re, v_im = v
        for s in range(steps):
            r = group(k * steps + s)
            l_re = m_ref[0, 0]
            l_im = m_ref[0, 1]
            v_re, v_im = (l_re * v_re - l_im * v_im + x_ref[r, 0:ns],
                          l_re * v_im + l_im * v_re + x_ref[r, ns:2 * ns])
            x_ref[r, 0:ns] = v_re
            x_ref[r, ns:2 * ns] = v_im
        return v_re, v_im

    def carry_in(e):
        e_re, e_im = e
        for n, shift in enumerate((1, 2, 4)):
            a_re = m_ref[0, 2 + 2 * n]
            a_im = m_ref[0, 3 + 2 * n]
            s_re = pltpu.roll(e_re, shift, 0)
            s_im = pltpu.roll(e_im, shift, 0)
            e_re, e_im = (e_re + (a_re * s_re - a_im * s_im),
                          e_im + (a_re * s_im + a_im * s_re))
        not_first = m_ref[0, 8]
        return not_first * pltpu.roll(e_re, 1, 0), not_first * pltpu.roll(e_im, 1, 0)

    def fix(k, w):
        w_re, w_im = w
        for s in range(0, steps, 2):
            parts = []
            for i in (k * steps + s, k * steps + s + 1):
                l_re = m_ref[0, 0]
                l_im = m_ref[0, 1]
                w_re, w_im = l_re * w_re - l_im * w_im, l_re * w_im + l_im * w_re
                parts.append(jnp.concatenate([x_ref[group(i), 0:ns] + w_re,
                                              x_ref[group(i), ns:2 * ns] + w_im], axis=1))
            i0 = k * steps + s
            xb_ref[i0 * SUBLANES:(i0 + 2) * SUBLANES, :] = jnp.concatenate(parts, axis=0).astype(BF16)
        return w_re, w_im

    def readout(k):
        y = _dot(xb_ref[rows(k), :], c_ref[0]) + d_ref[...] * u_ref[rows(k), :].astype(F32)
        o_ref[rows(k), :] = jax.nn.gelu(y, approximate=True).astype(o_ref.dtype)

    project(0)
    if nchunks > 1:
        project(1)
    zero = jnp.zeros((SUBLANES, ns), F32)
    v = scan(0, (zero, zero))
    for k in range(1, nchunks):
        if k + 1 < nchunks:
            project(k + 1)
        v = scan(k, v)
    w = fix(0, carry_in(v))
    for k in range(1, nchunks):
        w = fix(k, w)
        readout(k - 1)
    readout(nchunks - 1)


def _ssm_operands(lam_re, lam_im, log_dt, b_re, b_im, c_re, c_im, sub_len):
    g = lam_re.shape[0]
    nslab = g // SLAB_GROUPS
    dt = jnp.exp(log_dt)[:, None]
    lam = lax.complex(lam_re, lam_im)
    lam_bar = jnp.exp(lam * dt)
    b_bar = ((lam_bar - 1.0) / lam)[..., None] * lax.complex(b_re, b_im)
    eye = jnp.eye(SLAB_GROUPS, dtype=F32)

    def b_block(part):
        t = part.reshape(nslab, SLAB_GROUPS, SSM_STATE, SSM_GROUP).transpose(0, 1, 3, 2)
        return jnp.einsum("sgcp,gh->sgchp", t, eye).reshape(nslab, SLAB, SLAB_STATES)

    def c_block(part):
        t = part.reshape(nslab, SLAB_GROUPS, SSM_GROUP, SSM_STATE).transpose(0, 1, 3, 2)
        return jnp.einsum("sgpc,gh->sgphc", t, eye).reshape(nslab, SLAB_STATES, SLAB)

    b_mat = jnp.concatenate([b_block(b_bar.real), b_block(b_bar.imag)], axis=2).astype(BF16)
    c_mat = jnp.concatenate([c_block(c_re), c_block(-c_im)], axis=1).astype(BF16)

    lam1 = lam_bar.reshape(nslab, 1, SLAB_STATES)
    big1, sq, n = jnp.ones_like(lam1), lam1, sub_len
    while n:
        if n & 1:
            big1 = big1 * sq
        sq, n = sq * sq, n >> 1
    big2 = big1 * big1
    big4 = big2 * big2
    rows = jnp.arange(SUBLANES)[None, :, None]
    every = jnp.broadcast_to(lam1, (nslab, SUBLANES, SLAB_STATES))
    planes = [every.real, every.imag]
    for big, shift in ((big1, 1), (big2, 2), (big4, 4)):
        a = jnp.where(rows >= shift, big, 0.0)
        planes += [a.real, a.imag]
    planes.append(jnp.broadcast_to((rows >= 1).astype(F32), (nslab, SUBLANES, SLAB_STATES)))
    mult = jnp.stack(planes, axis=1).astype(F32)
    return b_mat, mult, c_mat


def _ssm(u_perm, b_mat, mult, c_mat, d_skip, *, batch, seq):
    nslab = b_mat.shape[0]
    width = nslab * SLAB
    sub_len = seq // SUBLANES
    assert seq % SSM_CHUNK == 0 and (SSM_CHUNK // SUBLANES) % 2 == 0
    return pl.pallas_call(
        functools.partial(_ssm_kernel, seq=seq),
        grid=(batch, nslab),
        in_specs=[pl.BlockSpec((seq, SLAB), lambda b, s: (b, s)),
                  pl.BlockSpec((1, SLAB, 2 * SLAB_STATES), lambda b, s: (s, 0, 0)),
                  pl.BlockSpec((1, 9, SUBLANES, SLAB_STATES), lambda b, s: (s, 0, 0, 0)),
                  pl.BlockSpec((1, 2 * SLAB_STATES, SLAB), lambda b, s: (s, 0, 0)),
                  pl.BlockSpec((1, SLAB), lambda b, s: (0, s))],
        out_specs=pl.BlockSpec((seq, SLAB), lambda b, s: (b, s)),
        out_shape=jax.ShapeDtypeStruct((batch * seq, width), BF16),
        scratch_shapes=[pltpu.VMEM((seq, 2 * SLAB_STATES), F32), pltpu.VMEM((seq, 2 * SLAB_STATES), BF16)],
        compiler_params=_params("parallel", "parallel"),
        name="s5_ssm",
    )(u_perm, b_mat, mult, c_mat, d_skip.reshape(1, width))


def kernel(x, p, g_mix, w_in, w_br_attn, lam_re, lam_im, log_dt, b_re, b_im, c_re, c_im, d_skip, w_glu, w_br_ssm, w_o, g_mlp, w_ff1, w_ff2, g_ple, w_ple_gate, w_ple, g_final):
    bsz, seq, d = x.shape
    depth = w_in.shape[0]
    attn_w = w_br_attn.shape[1]
    ssm_w = w_br_ssm.shape[1]
    heads = attn_w // HEAD_DIM
    t = bsz * seq
    assert w_in.shape[2] == 3 * attn_w + ssm_w + 2 * d

    wb_in = _to_bf16(w_in, g_mix, layer=0)
    wb_br_attn = _to_bf16(w_br_attn)
    wb_glu = _to_bf16(w_glu)
    wb_br_ssm = _to_bf16(w_br_ssm)
    wb_o = _to_bf16(w_o)
    wb_ple_gate = _to_bf16(w_ple_gate, g_ple)
    wb_ple = _to_bf16(w_ple)
    no_gain = jnp.ones(w_ff2.shape[:2], F32)

    h = x.reshape(t, d)
    hb, ss = _enter(h)
    for i in range(depth):
        qkv, gates = _proj(hb, ss, wb_in, i, attn_w=attn_w, ssm_w=ssm_w, d_model=d)
        u_perm = _proj_u(hb, ss, wb_in, i, col0=3 * attn_w, width=ssm_w, batch=bsz, seq=seq)
        attn = _attention(qkv, batch=bsz, seq=seq, heads=heads)
        b_mat, mult, c_mat = _ssm_operands(lam_re[i], lam_im[i], log_dt[i], b_re[i], b_im[i],
                                           c_re[i], c_im[i], seq // SUBLANES)
        y_perm = _ssm(u_perm, b_mat, mult, c_mat, d_skip[i], batch=bsz, seq=seq)
        ssm_perm = _glu(y_perm, wb_glu, i)
        merged = _merge(attn, wb_br_attn, ssm_perm, wb_br_ssm, gates, i, batch=bsz, seq=seq)
        h, hb, ss, wb_ff1 = _resid_matmul(merged, wb_o, i, h, name="out_proj",
                                          side=_SideCast(w_ff1, g_mlp, i))

        act, wb_ff2 = _normed_relu2_matmul(hb, ss, wb_ff1, i, name="ff1",
                                           side=_SideCast(w_ff2, no_gain, i))
        if i + 1 < depth:
            h, hb, ss, wb_in = _resid_matmul(act, wb_ff2, i, h, name="ff2",
                                             side=_SideCast(w_in, g_mix, i + 1))
        else:
            h, hb, ss = _resid_matmul(act, wb_ff2, i, h, name="ff2")

        h, hb, ss = _ple(hb, ss, wb_ple_gate, p[i].reshape(t, -1), wb_ple, i, h)
    return _rmsnorm(h, g_final, x.dtype).reshape(bsz, seq, d)
```

```python
import functools
import math
from typing import NamedTuple

import jax
import jax.numpy as jnp
from jax import lax
from jax.experimental import pallas as pl
from jax.experimental.pallas import tpu as pltpu

F32 = jnp.float32
BF16 = jnp.bfloat16

RMS_EPS = 1e-6
HEAD_DIM = 128
ATTN_Q_SCALE = HEAD_DIM ** -0.5 * math.log2(math.e)
ATTN_HEADS_PER_STEP = 8
SSM_GROUP = 16
SSM_STATE = 64
SLAB = 256
SLAB_GROUPS = SLAB // SSM_GROUP
SLAB_STATES = SLAB_GROUPS * SSM_STATE
SUBLANES = 8
LANES = 128
SSM_CHUNK = 256
V7X_VMEM_LIMIT = 56 * 1024 * 1024
CAST_BLOCK_BYTES = 8 * 1024 * 1024


def _tile(dim, pref):
    if dim % pref == 0:
        return pref
    assert dim < pref, (dim, pref)
    return dim


def _params(*sem):
    return pltpu.CompilerParams(dimension_semantics=sem, vmem_limit_bytes=V7X_VMEM_LIMIT)


def _dot(a, b):
    return jnp.dot(a, b, preferred_element_type=F32)


def _wspec(w, layer, k, tn, joff=0):
    if w.ndim == 2:
        return pl.BlockSpec((k, tn), lambda i, j, *kk: (kk[0] if kk else 0, j + joff))
    return pl.BlockSpec((None, k, tn), lambda i, j, *kk: (layer, kk[0] if kk else 0, j + joff))


class _SideCast(NamedTuple):
    w: jax.Array
    gain: jax.Array
    layer: int


def _side_cast_specs(side, nsteps, step_of):
    depth, k, n = side.w.shape
    assert k % nsteps == 0 and (k // nsteps) % 16 == 0, (k, nsteps)
    bk = k // nsteps
    in_specs = [pl.BlockSpec((None, bk, n), lambda *g: (side.layer, step_of(*g), 0)),
                pl.BlockSpec((None, bk, 1), lambda *g: (side.layer, step_of(*g), 0))]
    return in_specs, pl.BlockSpec((bk, n), lambda *g: (step_of(*g), 0)), jax.ShapeDtypeStruct((k, n), BF16)


def _cast_block(w_ref, g_ref, o_ref):
    o_ref[...] = (w_ref[...] * g_ref[...]).astype(o_ref.dtype)


def _rstd(ss_ref, d):
    return lax.rsqrt(ss_ref[...] * (1.0 / d) + RMS_EPS)


def _emit_stream(h, h_ref, hb_ref, ss_ref, j):
    h_ref[...] = h
    hb_ref[...] = h.astype(BF16)
    row_ss = jnp.sum(h * h, axis=1, keepdims=True)

    @pl.when(j == 0)
    def _():
        ss_ref[...] = row_ss

    @pl.when(j > 0)
    def _():
        ss_ref[...] += row_ss


def _stream_out(t, n, tm, tn):
    specs = [pl.BlockSpec((tm, tn), lambda i, j, *kk: (i, j)),
             pl.BlockSpec((tm, tn), lambda i, j, *kk: (i, j)),
             pl.BlockSpec((tm, 1), lambda i, j, *kk: (i, 0))]
    shapes = [jax.ShapeDtypeStruct((t, n), F32), jax.ShapeDtypeStruct((t, n), BF16),
              jax.ShapeDtypeStruct((t, 1), F32)]
    return specs, shapes


def _to_bf16(w, gain=None, layer=None):
    depth, k, n = w.shape
    bk = max(16, min(k, CAST_BLOCK_BYTES // (4 * n)))
    assert k % bk == 0, (k, bk)
    if gain is None:
        gain = jnp.ones((depth, k), F32)
    if layer is None:
        grid, first = (depth, k // bk), 0
        out_spec = pl.BlockSpec((None, bk, n), lambda a, r: (a, r, 0))
        out_shape = jax.ShapeDtypeStruct(w.shape, BF16)
    else:
        grid, first = (1, k // bk), layer
        out_spec = pl.BlockSpec((bk, n), lambda a, r: (r, 0))
        out_shape = jax.ShapeDtypeStruct((k, n), BF16)
    return pl.pallas_call(
        _cast_block,
        grid=grid,
        in_specs=[pl.BlockSpec((None, bk, n), lambda a, r: (a + first, r, 0)),
                  pl.BlockSpec((None, bk, 1), lambda a, r: (a + first, r, 0))],
        out_specs=out_spec,
        out_shape=out_shape,
        compiler_params=_params("parallel", "parallel"),
        name="cast_bf16",
    )(w, gain.reshape(depth, k, 1))


def _enter_kernel(x_ref, xb_ref, ss_ref):
    x = x_ref[...]
    xb_ref[...] = x.astype(BF16)
    ss_ref[...] = jnp.sum(x * x, axis=1, keepdims=True)


def _enter(x):
    t, d = x.shape
    tm = _tile(t, 256)
    return pl.pallas_call(
        _enter_kernel,
        grid=(t // tm,),
        in_specs=[pl.BlockSpec((tm, d), lambda i: (i, 0))],
        out_specs=[pl.BlockSpec((tm, d), lambda i: (i, 0)), pl.BlockSpec((tm, 1), lambda i: (i, 0))],
        out_shape=[jax.ShapeDtypeStruct((t, d), BF16), jax.ShapeDtypeStruct((t, 1), F32)],
        compiler_params=_params("parallel"),
        name="enter",
    )(x)


def _rmsnorm_kernel(x_ref, g_ref, o_ref):
    x = x_ref[...]
    ms = jnp.mean(x * x, axis=-1, keepdims=True)
    o_ref[...] = (x * lax.rsqrt(ms + RMS_EPS) * g_ref[...]).astype(o_ref.dtype)


def _rmsnorm(x, g, out_dtype):
    t, d = x.shape
    tm = _tile(t, 256)
    return pl.pallas_call(
        _rmsnorm_kernel,
        grid=(t // tm,),
        in_specs=[pl.BlockSpec((tm, d), lambda i: (i, 0)),
                  pl.BlockSpec((1, d), lambda i: (0, 0))],
        out_specs=pl.BlockSpec((tm, d), lambda i: (i, 0)),
        out_shape=jax.ShapeDtypeStruct((t, d), out_dtype),
        compiler_params=_params("parallel"),
        name="rmsnorm",
    )(x, g.reshape(1, d))


def _normed_mm_kernel(hb_ref, w_ref, ss_ref, *rest, d, side):
    if side:
        sw_ref, sg_ref, o_ref, so_ref = rest
        _cast_block(sw_ref, sg_ref, so_ref)
    else:
        (o_ref,) = rest
    acc = _dot(hb_ref[...], w_ref[...]) * _rstd(ss_ref, d)
    o_ref[...] = jnp.square(jnp.maximum(acc, 0.0)).astype(o_ref.dtype)


def _normed_relu2_matmul(hb, ss, w, layer, *, name, side=None):
    t, k = hb.shape
    ncols = w.shape[-1]
    tm, tn = _tile(t, 1024), _tile(ncols, 1024)
    nj = ncols // tn
    in_specs = [pl.BlockSpec((tm, k), lambda i, j: (i, 0)),
                _wspec(w, layer, k, tn),
                pl.BlockSpec((tm, 1), lambda i, j: (i, 0))]
    args = [hb, w, ss]
    out_specs = [pl.BlockSpec((tm, tn), lambda i, j: (i, j))]
    out_shape = [jax.ShapeDtypeStruct((t, ncols), BF16)]
    if side is not None:
        s_in, s_out, s_shape = _side_cast_specs(side, (t // tm) * nj, lambda i, j: i * nj + j)
        in_specs += s_in
        args += [side.w, side.gain.reshape(*side.gain.shape, 1)]
        out_specs.append(s_out)
        out_shape.append(s_shape)
    outs = pl.pallas_call(
        functools.partial(_normed_mm_kernel, d=k, side=side is not None),
        grid=(t // tm, nj),
        in_specs=in_specs,
        out_specs=out_specs,
        out_shape=out_shape,
        compiler_params=_params("parallel", "parallel"),
        name=name,
    )(*args)
    return outs if side is not None else outs[0]


def _proj_kernel(hb_ref, w_ref, ss_ref, qkv_ref, gates_ref, *, d, nq, nqkv):
    j = pl.program_id(1)

    @pl.when(j < nqkv)
    def _():
        scale = jnp.where(j < nq, ATTN_Q_SCALE, 1.0)
        qkv_ref[...] = (_dot(hb_ref[...], w_ref[...]) * (_rstd(ss_ref, d) * scale)).astype(qkv_ref.dtype)

    @pl.when(j >= nqkv)
    def _():
        gates_ref[...] = jax.nn.sigmoid(_dot(hb_ref[...], w_ref[...]) * _rstd(ss_ref, d)).astype(gates_ref.dtype)


def _proj(hb, ss, w, layer, *, attn_w, ssm_w, d_model):
    t, k = hb.shape
    tm = _tile(t, 1024)
    tn = next(c for c in (1024, 512, 256, 128)
              if attn_w % c == 0 and ssm_w % c == 0 and (2 * d_model) % c == 0)
    nq, nqkv, nu, ng = attn_w // tn, 3 * attn_w // tn, ssm_w // tn, 2 * d_model // tn

    def wcol(j):
        return jnp.where(j < nqkv, j, j + nu)

    if w.ndim == 2:
        w_spec = pl.BlockSpec((k, tn), lambda i, j: (0, wcol(j)))
    else:
        w_spec = pl.BlockSpec((None, k, tn), lambda i, j: (layer, 0, wcol(j)))
    return pl.pallas_call(
        functools.partial(_proj_kernel, d=k, nq=nq, nqkv=nqkv),
        grid=(t // tm, nqkv + ng),
        in_specs=[pl.BlockSpec((tm, k), lambda i, j: (i, 0)),
                  w_spec,
                  pl.BlockSpec((tm, 1), lambda i, j: (i, 0))],
        out_specs=[pl.BlockSpec((tm, tn), lambda i, j: (i, jnp.minimum(j, nqkv - 1))),
                   pl.BlockSpec((tm, tn), lambda i, j: (i, jnp.maximum(j - nqkv, 0)))],
        out_shape=[jax.ShapeDtypeStruct((t, 3 * attn_w), BF16),
                   jax.ShapeDtypeStruct((t, 2 * d_model), BF16)],
        compiler_params=_params("parallel", "arbitrary"),
        name="proj",
    )(hb, w, ss)


def _proj_u_kernel(hb_ref, w_ref, ss_ref, o_ref, stage_ref, *, d, sub_len):
    acc = _dot(hb_ref[...], w_ref[...]) * _rstd(ss_ref, d)
    for c in range(acc.shape[1] // LANES):
        for r in range(SUBLANES):
            stage_ref[c, pl.ds(r, sub_len, stride=SUBLANES), :] = (
                acc[r * sub_len:(r + 1) * sub_len, c * LANES:(c + 1) * LANES])
    for c in range(acc.shape[1] // LANES):
        o_ref[:, c * LANES:(c + 1) * LANES] = stage_ref[c].astype(o_ref.dtype)


def _proj_u(hb, ss, w, layer, *, col0, width, batch, seq):
    t, k = hb.shape
    sub_len = seq // SUBLANES
    tn = _tile(width, 256)
    assert col0 % tn == 0
    joff = col0 // tn
    if w.ndim == 2:
        w_spec = pl.BlockSpec((k, tn), lambda b, j: (0, j + joff))
    else:
        w_spec = pl.BlockSpec((None, k, tn), lambda b, j: (layer, 0, j + joff))
    return pl.pallas_call(
        functools.partial(_proj_u_kernel, d=k, sub_len=sub_len),
        grid=(batch, width // tn),
        in_specs=[pl.BlockSpec((seq, k), lambda b, j: (b, 0)),
                  w_spec,
                  pl.BlockSpec((seq, 1), lambda b, j: (b, 0))],
        out_specs=pl.BlockSpec((seq, tn), lambda b, j: (b, j)),
        out_shape=jax.ShapeDtypeStruct((t, width), BF16),
        scratch_shapes=[pltpu.VMEM((tn // LANES, seq, LANES), F32)],
        compiler_params=_params("parallel", "parallel"),
        name="proj_u",
    )(hb, w, ss)


def _glu_kernel(y_ref, w_ref, o_ref, stage_ref, *, width):
    y = y_ref[...]
    res = y.astype(F32) * jax.nn.sigmoid(_dot(y, w_ref[...]))
    ntile = width // LANES
    sub_rows = res.shape[0] // SUBLANES
    for c in range(ntile):
        stage_ref[c] = res[:, c * LANES:(c + 1) * LANES]
    for r in range(SUBLANES):
        for c in range(ntile):
            lane0 = r * width + c * LANES
            o_ref[:, lane0:lane0 + LANES] = (
                stage_ref[c, pl.ds(r, sub_rows, stride=SUBLANES), :].astype(o_ref.dtype))


def _glu(y_perm, w, layer, *, batch, seq):
    t, width = y_perm.shape
    assert w.shape[-1] == width and w.shape[-2] == width
    tm = _tile(seq, 512)
    tiles = seq // tm
    return pl.pallas_call(
        functools.partial(_glu_kernel, width=width),
        grid=(t // tm,),
        in_specs=[pl.BlockSpec((tm, width), lambda i: (i, 0)),
                  pl.BlockSpec((None, width, width), lambda i: (layer, 0, 0))],
        out_specs=pl.BlockSpec((None, tm // SUBLANES, SUBLANES * width), lambda i: (i // tiles, i % tiles, 0)),
        out_shape=jax.ShapeDtypeStruct((batch, seq // SUBLANES, SUBLANES * width), BF16),
        scratch_shapes=[pltpu.VMEM((width // LANES, tm, LANES), F32)],
        compiler_params=_params("parallel"),
        name="glu",
    )(y_perm, w)


def _merge_kernel(a_ref, wa_ref, s_ref, ws_ref, ga_ref, gs_ref, o_ref, *, nr, sub_len, width):
    up_a = _dot(a_ref[...], wa_ref[...])
    for rl in range(nr):
        rows = slice(rl * sub_len, (rl + 1) * sub_len)
        up_s = _dot(s_ref[:, rl * width:(rl + 1) * width], ws_ref[...])
        o_ref[rows, :] = (ga_ref[rows, :].astype(F32) * up_a[rows, :]
                          + gs_ref[rows, :].astype(F32) * up_s).astype(o_ref.dtype)


def _merge(attn, wa, ssm3, ws, gates, layer, *, seq):
    t, ka = attn.shape
    sub_len = seq // SUBLANES
    assert ssm3.shape[1] == sub_len
    ks = ssm3.shape[2] // SUBLANES
    d = wa.shape[-1]
    tm, tn = _tile(seq, 1024), _tile(d, 1024)
    assert tm % sub_len == 0
    nr = tm // sub_len
    tiles = seq // tm
    nj = d // tn
    return pl.pallas_call(
        functools.partial(_merge_kernel, nr=nr, sub_len=sub_len, width=ks),
        grid=(t // tm, nj),
        in_specs=[pl.BlockSpec((tm, ka), lambda i, j: (i, 0)),
                  _wspec(wa, layer, ka, tn),
                  pl.BlockSpec((None, sub_len, nr * ks), lambda i, j: (i // tiles, 0, i % tiles)),
                  _wspec(ws, layer, ks, tn),
                  pl.BlockSpec((tm, tn), lambda i, j: (i, j)),
                  pl.BlockSpec((tm, tn), lambda i, j: (i, j + nj))],
        out_specs=pl.BlockSpec((tm, tn), lambda i, j: (i, j)),
        out_shape=jax.ShapeDtypeStruct((t, d), BF16),
        compiler_params=_params("parallel", "parallel"),
        name="merge",
    )(attn, wa, ssm3, ws, gates, gates)


def _resid_mm_kernel(x_ref, w_ref, h_ref, *rest, nk, side):
    if side:
        sw_ref, sg_ref, o_ref, ob_ref, ss_ref, so_ref = rest
        _cast_block(sw_ref, sg_ref, so_ref)
    else:
        o_ref, ob_ref, ss_ref = rest
    j = pl.program_id(1)
    if nk == 1:
        _emit_stream(h_ref[...] + _dot(x_ref[...], w_ref[...]), o_ref, ob_ref, ss_ref, j)
        return
    k = pl.program_id(2)

    @pl.when(k == 0)
    def _():
        o_ref[...] = h_ref[...] + _dot(x_ref[...], w_ref[...])

    @pl.when(jnp.logical_and(k > 0, k < nk - 1))
    def _():
        o_ref[...] += _dot(x_ref[...], w_ref[...])

    @pl.when(k == nk - 1)
    def _():
        _emit_stream(o_ref[...] + _dot(x_ref[...], w_ref[...]), o_ref, ob_ref, ss_ref, j)


def _resid_matmul(x, w, layer, h, *, name, side=None):
    t, k = x.shape
    n = w.shape[-1]
    tk = k if k <= 4096 else _tile(k, 2048)
    nk = k // tk
    assert nk == 1 or nk >= 2
    tm, tn = _tile(t, 1024), _tile(n, 512 if nk == 1 else 1024)
    nj = n // tn
    in_specs = [pl.BlockSpec((tm, tk), lambda i, j, kk: (i, kk)),
                _wspec(w, layer, tk, tn),
                pl.BlockSpec((tm, tn), lambda i, j, kk: (i, j))]
    args = [x, w, h]
    out_specs, out_shape = _stream_out(t, n, tm, tn)
    if side is not None:
        s_in, s_out, s_shape = _side_cast_specs(side, (t // tm) * nj * nk,
                                                lambda i, j, kk: (i * nj + j) * nk + kk)
        in_specs += s_in
        args += [side.w, side.gain.reshape(*side.gain.shape, 1)]
        out_specs.append(s_out)
        out_shape.append(s_shape)
    return pl.pallas_call(
        functools.partial(_resid_mm_kernel, nk=nk, side=side is not None),
        grid=(t // tm, nj, nk),
        in_specs=in_specs,
        out_specs=out_specs,
        out_shape=out_shape,
        compiler_params=_params("parallel", "arbitrary", "arbitrary"),
        name=name,
    )(*args)


def _ple_kernel(hb_ref, wg_ref, ss_ref, p_ref, wp_ref, h_ref, o_ref, ob_ref, sso_ref, *, d):
    gate = jax.nn.sigmoid(_dot(hb_ref[...], wg_ref[...]) * _rstd(ss_ref, d))
    emb = _dot(p_ref[...].astype(BF16), wp_ref[...])
    _emit_stream(h_ref[...] + emb * gate, o_ref, ob_ref, sso_ref, pl.program_id(1))


def _ple(hb, ss, wg, p, wp, layer, h):
    t, k = hb.shape
    kp = p.shape[1]
    n = wg.shape[-1]
    tm, tn = _tile(t, 1024), _tile(n, 512)
    out_specs, out_shape = _stream_out(t, n, tm, tn)
    return pl.pallas_call(
        functools.partial(_ple_kernel, d=k),
        grid=(t // tm, n // tn),
        in_specs=[pl.BlockSpec((tm, k), lambda i, j: (i, 0)),
                  _wspec(wg, layer, k, tn),
                  pl.BlockSpec((tm, 1), lambda i, j: (i, 0)),
                  pl.BlockSpec((tm, kp), lambda i, j: (i, 0)),
                  _wspec(wp, layer, kp, tn),
                  pl.BlockSpec((tm, tn), lambda i, j: (i, j))],
        out_specs=out_specs,
        out_shape=out_shape,
        compiler_params=_params("parallel", "arbitrary"),
        name="ple",
    )(hb, wg, ss, p, wp, h)


def _attn_kernel(q_ref, k_ref, v_ref, tri_ref, sw_ref, sg_ref, o_ref, so_ref, acc_ref, carry_ref, *, tq, nheads):
    _cast_block(sw_ref, sg_ref, so_ref)
    qi = pl.program_id(2)
    tri = tri_ref[...]
    row = lax.broadcasted_iota(jnp.int32, (tq, tq), 0)
    col = lax.broadcasted_iota(jnp.int32, (tq, tq), 1)
    causal = col < row

    def key_block(kb, masked):
        start = pl.multiple_of(kb * tq, tq)
        heads = range(nheads)
        lanes = [slice(hd * HEAD_DIM, (hd + 1) * HEAD_DIM) for hd in heads]
        zs = [lax.dot_general(q_ref[:, lanes[hd]], k_ref[pl.ds(start, tq), lanes[hd]],
                              (((1,), (1,)), ((), ())), preferred_element_type=F32) for hd in heads]
        tails = []
        for z in zs:
            neg_abs = lax.bitcast_convert_type(
                lax.bitcast_convert_type(z, jnp.uint32) | jnp.uint32(0x80000000), F32)
            softplus = jnp.maximum(z, 0.0) + jnp.log2(1.0 + jnp.exp2(neg_abs))
            if masked:
                softplus = jnp.where(causal, softplus, 0.0)
            hi = softplus.astype(BF16)
            lo = (softplus - hi.astype(F32)).astype(BF16)
            tails.append(_dot(jnp.concatenate([hi, lo], axis=1), tri))
        for hd in heads:
            w = jnp.exp2(zs[hd] + tails[hd] + carry_ref[hd])
            if masked:
                w = jnp.where(causal, w, 0.0)
            acc_ref[:, lanes[hd]] += _dot(w.astype(BF16), v_ref[pl.ds(start, tq), lanes[hd]])
            carry_ref[hd] += tails[hd][:, 0:1]

    acc_ref[...] = jnp.zeros_like(acc_ref)
    carry_ref[...] = jnp.zeros_like(carry_ref)
    key_block(qi, True)

    def body(i, c):
        key_block(qi - 1 - i, False)
        return c

    lax.fori_loop(0, qi, body, 0)
    o_ref[...] = acc_ref[...].astype(o_ref.dtype)


def _attention(qkv, side, *, batch, seq, heads):
    tq = _tile(seq, 256)
    nq = seq // tq
    nheads = _tile(heads, ATTN_HEADS_PER_STEP)
    hgroups = heads // nheads
    width = nheads * HEAD_DIM
    r = jnp.arange(tq)
    tri = -(r[:, None] >= r[None, :]).astype(BF16)
    tri2 = jnp.concatenate([tri, tri], axis=0)
    s_in, s_out, s_shape = _side_cast_specs(side, batch * hgroups * nq,
                                            lambda b, h, i: (b * hgroups + h) * nq + i)
    return pl.pallas_call(
        functools.partial(_attn_kernel, tq=tq, nheads=nheads),
        grid=(batch, hgroups, nq),
        in_specs=[pl.BlockSpec((tq, width), lambda b, h, i: (b * nq + i, h)),
                  pl.BlockSpec((seq, width), lambda b, h, i: (b, hgroups + h)),
                  pl.BlockSpec((seq, width), lambda b, h, i: (b, 2 * hgroups + h)),
                  pl.BlockSpec((2 * tq, tq), lambda b, h, i: (0, 0))] + s_in,
        out_specs=[pl.BlockSpec((tq, width), lambda b, h, i: (b * nq + i, h)), s_out],
        out_shape=[jax.ShapeDtypeStruct((batch * seq, heads * HEAD_DIM), BF16), s_shape],
        scratch_shapes=[pltpu.VMEM((tq, width), F32), pltpu.VMEM((nheads, tq, 1), F32)],
        compiler_params=_params("parallel", "parallel", "parallel"),
        name="stickbreak_attn",
    )(qkv, qkv, qkv, tri2, side.w, side.gain.reshape(*side.gain.shape, 1))


def _ssm_kernel(u_ref, b_ref, m_ref, c_ref, d_ref, o_ref, x_ref, xb_ref, *, seq):
    ns = SLAB_STATES
    nchunks = seq // SSM_CHUNK
    steps = SSM_CHUNK // SUBLANES

    def rows(k):
        return slice(k * SSM_CHUNK, (k + 1) * SSM_CHUNK)

    def group(i):
        return slice(i * SUBLANES, (i + 1) * SUBLANES)

    def project(k):
        x_ref[rows(k), :] = _dot(u_ref[rows(k), :], b_ref[0])

    def scan(k, v):
        v_re, v_im = v
        for s in range(steps):
            r = group(k * steps + s)
            l_re = m_ref[0, 0]
            l_im = m_ref[0, 1]
            v_re, v_im = (l_re * v_re - l_im * v_im + x_ref[r, 0:ns],
                          l_re * v_im + l_im * v_re + x_ref[r, ns:2 * ns])
            x_ref[r, 0:ns] = v_re
            x_ref[r, ns:2 * ns] = v_im
        return v_re, v_im

    def carry_in(e):
        e_re, e_im = e
        for n, shift in enumerate((1, 2, 4)):
            a_re = m_ref[0, 2 + 2 * n]
            a_im = m_ref[0, 3 + 2 * n]
            s_re = pltpu.roll(e_re, shift, 0)
            s_im = pltpu.roll(e_im, shift, 0)
            e_re, e_im = (e_re + (a_re * s_re - a_im * s_im),
                          e_im + (a_re * s_im + a_im * s_re))
        not_first = m_ref[0, 8]
        return not_first * pltpu.roll(e_re, 1, 0), not_first * pltpu.roll(e_im, 1, 0)

    def fix(k, w):
        w_re, w_im = w
        for s in range(0, steps, 2):
            parts = []
            for i in (k * steps + s, k * steps + s + 1):
                l_re = m_ref[0, 0]
                l_im = m_ref[0, 1]
                w_re, w_im = l_re * w_re - l_im * w_im, l_re * w_im + l_im * w_re
                parts.append(jnp.concatenate([x_ref[group(i), 0:ns] + w_re,
                                              x_ref[group(i), ns:2 * ns] + w_im], axis=1))
            i0 = k * steps + s
            xb_ref[i0 * SUBLANES:(i0 + 2) * SUBLANES, :] = jnp.concatenate(parts, axis=0).astype(BF16)
        return w_re, w_im

    def readout(k):
        y = _dot(xb_ref[rows(k), :], c_ref[0]) + d_ref[...] * u_ref[rows(k), :].astype(F32)
        o_ref[rows(k), :] = jax.nn.gelu(y, approximate=True).astype(o_ref.dtype)

    project(0)
    if nchunks > 1:
        project(1)
    zero = jnp.zeros((SUBLANES, ns), F32)
    v = scan(0, (zero, zero))
    for k in range(1, nchunks):
        if k + 1 < nchunks:
            project(k + 1)
        v = scan(k, v)
    w = fix(0, carry_in(v))
    for k in range(1, nchunks):
        w = fix(k, w)
        readout(k - 1)
    readout(nchunks - 1)


def _ssm_operands(lam_re, lam_im, log_dt, b_re, b_im, c_re, c_im, sub_len):
    g = lam_re.shape[0]
    nslab = g // SLAB_GROUPS
    dt = jnp.exp(log_dt)[:, None]
    lam = lax.complex(lam_re, lam_im)
    lam_bar = jnp.exp(lam * dt)
    b_bar = ((lam_bar - 1.0) / lam)[..., None] * lax.complex(b_re, b_im)
    eye = jnp.eye(SLAB_GROUPS, dtype=F32)

    def b_block(part):
        t = part.reshape(nslab, SLAB_GROUPS, SSM_STATE, SSM_GROUP).transpose(0, 1, 3, 2)
        return jnp.einsum("sgcp,gh->sgchp", t, eye).reshape(nslab, SLAB, SLAB_STATES)

    def c_block(part):
        t = part.reshape(nslab, SLAB_GROUPS, SSM_GROUP, SSM_STATE).transpose(0, 1, 3, 2)
        return jnp.einsum("sgpc,gh->sgphc", t, eye).reshape(nslab, SLAB_STATES, SLAB)

    b_mat = jnp.concatenate([b_block(b_bar.real), b_block(b_bar.imag)], axis=2).astype(BF16)
    c_mat = jnp.concatenate([c_block(c_re), c_block(-c_im)], axis=1).astype(BF16)

    lam1 = lam_bar.reshape(nslab, 1, SLAB_STATES)
    big1, sq, n = jnp.ones_like(lam1), lam1, sub_len
    while n:
        if n & 1:
            big1 = big1 * sq
        sq, n = sq * sq, n >> 1
    big2 = big1 * big1
    big4 = big2 * big2
    rows = jnp.arange(SUBLANES)[None, :, None]
    every = jnp.broadcast_to(lam1, (nslab, SUBLANES, SLAB_STATES))
    planes = [every.real, every.imag]
    for big, shift in ((big1, 1), (big2, 2), (big4, 4)):
        a = jnp.where(rows >= shift, big, 0.0)
        planes += [a.real, a.imag]
    planes.append(jnp.broadcast_to((rows >= 1).astype(F32), (nslab, SUBLANES, SLAB_STATES)))
    mult = jnp.stack(planes, axis=1).astype(F32)
    return b_mat, mult, c_mat


def _ssm(u_perm, b_mat, mult, c_mat, d_skip, *, batch, seq):
    nslab = b_mat.shape[0]
    width = nslab * SLAB
    sub_len = seq // SUBLANES
    assert seq % SSM_CHUNK == 0 and (SSM_CHUNK // SUBLANES) % 2 == 0
    return pl.pallas_call(
        functools.partial(_ssm_kernel, seq=seq),
        grid=(batch, nslab),
        in_specs=[pl.BlockSpec((seq, SLAB), lambda b, s: (b, s)),
                  pl.BlockSpec((1, SLAB, 2 * SLAB_STATES), lambda b, s: (s, 0, 0)),
                  pl.BlockSpec((1, 9, SUBLANES, SLAB_STATES), lambda b, s: (s, 0, 0, 0)),
                  pl.BlockSpec((1, 2 * SLAB_STATES, SLAB), lambda b, s: (s, 0, 0)),
                  pl.BlockSpec((1, SLAB), lambda b, s: (0, s))],
        out_specs=pl.BlockSpec((seq, SLAB), lambda b, s: (b, s)),
        out_shape=jax.ShapeDtypeStruct((batch * seq, width), BF16),
        scratch_shapes=[pltpu.VMEM((seq, 2 * SLAB_STATES), F32), pltpu.VMEM((seq, 2 * SLAB_STATES), BF16)],
        compiler_params=_params("parallel", "parallel"),
        name="s5_ssm",
    )(u_perm, b_mat, mult, c_mat, d_skip.reshape(1, width))


def kernel(x, p, g_mix, w_in, w_br_attn, lam_re, lam_im, log_dt, b_re, b_im, c_re, c_im, d_skip, w_glu, w_br_ssm, w_o, g_mlp, w_ff1, w_ff2, g_ple, w_ple_gate, w_ple, g_final):
    bsz, seq, d = x.shape
    depth = w_in.shape[0]
    attn_w = w_br_attn.shape[1]
    ssm_w = w_br_ssm.shape[1]
    heads = attn_w // HEAD_DIM
    t = bsz * seq
    assert w_in.shape[2] == 3 * attn_w + ssm_w + 2 * d

    wb_in = _to_bf16(w_in, g_mix, layer=0)
    wb_br_attn = _to_bf16(w_br_attn)
    wb_glu = _to_bf16(w_glu)
    wb_br_ssm = _to_bf16(w_br_ssm)
    wb_o = _to_bf16(w_o)
    wb_ple_gate = _to_bf16(w_ple_gate, g_ple)
    wb_ple = _to_bf16(w_ple)
    no_gain = jnp.ones(w_ff2.shape[:2], F32)

    h = x.reshape(t, d)
    hb, ss = _enter(h)
    for i in range(depth):
        qkv, gates = _proj(hb, ss, wb_in, i, attn_w=attn_w, ssm_w=ssm_w, d_model=d)
        u_perm = _proj_u(hb, ss, wb_in, i, col0=3 * attn_w, width=ssm_w, batch=bsz, seq=seq)
        attn, wb_ff1 = _attention(qkv, _SideCast(w_ff1, g_mlp, i), batch=bsz, seq=seq, heads=heads)
        b_mat, mult, c_mat = _ssm_operands(lam_re[i], lam_im[i], log_dt[i], b_re[i], b_im[i],
                                           c_re[i], c_im[i], seq // SUBLANES)
        y_perm = _ssm(u_perm, b_mat, mult, c_mat, d_skip[i], batch=bsz, seq=seq)
        ssm3 = _glu(y_perm, wb_glu, i, batch=bsz, seq=seq)
        merged = _merge(attn, wb_br_attn, ssm3, wb_br_ssm, gates, i, seq=seq)
        h, hb, ss = _resid_matmul(merged, wb_o, i, h, name="out_proj")

        act, wb_ff2 = _normed_relu2_matmul(hb, ss, wb_ff1, i, name="ff1",
                                           side=_SideCast(w_ff2, no_gain, i))
        if i + 1 < depth:
            h, hb, ss, wb_in = _resid_matmul(act, wb_ff2, i, h, name="ff2",
                                             side=_SideCast(w_in, g_mix, i + 1))
        else:
            h, hb, ss = _resid_matmul(act, wb_ff2, i, h, name="ff2")

        h, hb, ss = _ple(hb, ss, wb_ple_gate, p[i].reshape(t, -1), wb_ple, i, h)
    return _rmsnorm(h, g_final, x.dtype).reshape(bsz, seq, d)
```

```python
import functools
import math
from typing import NamedTuple

import jax
import jax.numpy as jnp
from jax import lax
from jax.experimental import pallas as pl
from jax.experimental.pallas import tpu as pltpu

F32 = jnp.float32
BF16 = jnp.bfloat16

RMS_EPS = 1e-6
HEAD_DIM = 128
ATTN_Q_SCALE = HEAD_DIM ** -0.5 * math.log2(math.e)
ATTN_HEADS_PER_STEP = 8
SSM_GROUP = 16
SSM_STATE = 64
SLAB = 256
SLAB_GROUPS = SLAB // SSM_GROUP
SLAB_STATES = SLAB_GROUPS * SSM_STATE
SUBLANES = 8
LANES = 128
SSM_CHUNK = 256
V7X_VMEM_LIMIT = 56 * 1024 * 1024
CAST_BLOCK_BYTES = 8 * 1024 * 1024


def _tile(dim, pref):
    if dim % pref == 0:
        return pref
    assert dim < pref, (dim, pref)
    return dim


def _params(*sem):
    return pltpu.CompilerParams(dimension_semantics=sem, vmem_limit_bytes=V7X_VMEM_LIMIT)


def _dot(a, b):
    return jnp.dot(a, b, preferred_element_type=F32)


def _wspec(w, layer, k, tn, joff=0):
    if w.ndim == 2:
        return pl.BlockSpec((k, tn), lambda i, j, *kk: (kk[0] if kk else 0, j + joff))
    return pl.BlockSpec((None, k, tn), lambda i, j, *kk: (layer, kk[0] if kk else 0, j + joff))


class _SideCast(NamedTuple):
    w: jax.Array
    gain: jax.Array
    layer: int


def _side_cast_specs(side, nsteps, step_of):
    depth, k, n = side.w.shape
    nblk = max(b for b in range(1, nsteps + 1) if k % b == 0 and (k // b) % 16 == 0)
    bk = k // nblk

    def blk(*g):
        return jnp.minimum(step_of(*g), nblk - 1)

    in_specs = [pl.BlockSpec((None, bk, n), lambda *g: (side.layer, blk(*g), 0)),
                pl.BlockSpec((None, bk, 1), lambda *g: (side.layer, blk(*g), 0))]
    return in_specs, pl.BlockSpec((bk, n), lambda *g: (blk(*g), 0)), jax.ShapeDtypeStruct((k, n), BF16)


def _cast_block(w_ref, g_ref, o_ref):
    o_ref[...] = (w_ref[...] * g_ref[...]).astype(o_ref.dtype)


def _rstd(ss_ref, d):
    return lax.rsqrt(ss_ref[...] * (1.0 / d) + RMS_EPS)


def _emit_stream(h, h_ref, hb_ref, ss_ref, j):
    h_ref[...] = h
    hb_ref[...] = h.astype(BF16)
    row_ss = jnp.sum(h * h, axis=1, keepdims=True)

    @pl.when(j == 0)
    def _():
        ss_ref[...] = row_ss

    @pl.when(j > 0)
    def _():
        ss_ref[...] += row_ss


def _stream_out(t, n, tm, tn):
    specs = [pl.BlockSpec((tm, tn), lambda i, j, *kk: (i, j)),
             pl.BlockSpec((tm, tn), lambda i, j, *kk: (i, j)),
             pl.BlockSpec((tm, 1), lambda i, j, *kk: (i, 0))]
    shapes = [jax.ShapeDtypeStruct((t, n), F32), jax.ShapeDtypeStruct((t, n), BF16),
              jax.ShapeDtypeStruct((t, 1), F32)]
    return specs, shapes


def _to_bf16(w, gain=None, layer=None):
    depth, k, n = w.shape
    bk = max(16, min(k, CAST_BLOCK_BYTES // (4 * n)))
    assert k % bk == 0, (k, bk)
    if gain is None:
        gain = jnp.ones((depth, k), F32)
    if layer is None:
        grid, first = (depth, k // bk), 0
        out_spec = pl.BlockSpec((None, bk, n), lambda a, r: (a, r, 0))
        out_shape = jax.ShapeDtypeStruct(w.shape, BF16)
    else:
        grid, first = (1, k // bk), layer
        out_spec = pl.BlockSpec((bk, n), lambda a, r: (r, 0))
        out_shape = jax.ShapeDtypeStruct((k, n), BF16)
    return pl.pallas_call(
        _cast_block,
        grid=grid,
        in_specs=[pl.BlockSpec((None, bk, n), lambda a, r: (a + first, r, 0)),
                  pl.BlockSpec((None, bk, 1), lambda a, r: (a + first, r, 0))],
        out_specs=out_spec,
        out_shape=out_shape,
        compiler_params=_params("parallel", "parallel"),
        name="cast_bf16",
    )(w, gain.reshape(depth, k, 1))


def _enter_kernel(x_ref, xb_ref, ss_ref):
    x = x_ref[...]
    xb_ref[...] = x.astype(BF16)
    ss_ref[...] = jnp.sum(x * x, axis=1, keepdims=True)


def _enter(x):
    t, d = x.shape
    tm = _tile(t, 256)
    return pl.pallas_call(
        _enter_kernel,
        grid=(t // tm,),
        in_specs=[pl.BlockSpec((tm, d), lambda i: (i, 0))],
        out_specs=[pl.BlockSpec((tm, d), lambda i: (i, 0)), pl.BlockSpec((tm, 1), lambda i: (i, 0))],
        out_shape=[jax.ShapeDtypeStruct((t, d), BF16), jax.ShapeDtypeStruct((t, 1), F32)],
        compiler_params=_params("parallel"),
        name="enter",
    )(x)


def _rmsnorm_kernel(x_ref, g_ref, o_ref):
    x = x_ref[...]
    ms = jnp.mean(x * x, axis=-1, keepdims=True)
    o_ref[...] = (x * lax.rsqrt(ms + RMS_EPS) * g_ref[...]).astype(o_ref.dtype)


def _rmsnorm(x, g, out_dtype):
    t, d = x.shape
    tm = _tile(t, 256)
    return pl.pallas_call(
        _rmsnorm_kernel,
        grid=(t // tm,),
        in_specs=[pl.BlockSpec((tm, d), lambda i: (i, 0)),
                  pl.BlockSpec((1, d), lambda i: (0, 0))],
        out_specs=pl.BlockSpec((tm, d), lambda i: (i, 0)),
        out_shape=jax.ShapeDtypeStruct((t, d), out_dtype),
        compiler_params=_params("parallel"),
        name="rmsnorm",
    )(x, g.reshape(1, d))


def _normed_mm_kernel(hb_ref, w_ref, ss_ref, *rest, d, side):
    if side:
        sw_ref, sg_ref, o_ref, so_ref = rest
        _cast_block(sw_ref, sg_ref, so_ref)
    else:
        (o_ref,) = rest
    acc = _dot(hb_ref[...], w_ref[...]) * _rstd(ss_ref, d)
    o_ref[...] = jnp.square(jnp.maximum(acc, 0.0)).astype(o_ref.dtype)


def _normed_relu2_matmul(hb, ss, w, layer, *, name, side=None):
    t, k = hb.shape
    ncols = w.shape[-1]
    tm, tn = _tile(t, 1024), _tile(ncols, 1024)
    nj = ncols // tn
    in_specs = [pl.BlockSpec((tm, k), lambda i, j: (i, 0)),
                _wspec(w, layer, k, tn),
                pl.BlockSpec((tm, 1), lambda i, j: (i, 0))]
    args = [hb, w, ss]
    out_specs = [pl.BlockSpec((tm, tn), lambda i, j: (i, j))]
    out_shape = [jax.ShapeDtypeStruct((t, ncols), BF16)]
    if side is not None:
        s_in, s_out, s_shape = _side_cast_specs(side, (t // tm) * nj, lambda i, j: i * nj + j)
        in_specs += s_in
        args += [side.w, side.gain.reshape(*side.gain.shape, 1)]
        out_specs.append(s_out)
        out_shape.append(s_shape)
    outs = pl.pallas_call(
        functools.partial(_normed_mm_kernel, d=k, side=side is not None),
        grid=(t // tm, nj),
        in_specs=in_specs,
        out_specs=out_specs,
        out_shape=out_shape,
        compiler_params=_params("parallel", "parallel"),
        name=name,
    )(*args)
    return outs if side is not None else outs[0]


def _proj_kernel(hb_ref, w_ref, ss_ref, qkv_ref, gates_ref, *, d, nq, nqkv):
    j = pl.program_id(1)

    @pl.when(j < nqkv)
    def _():
        scale = jnp.where(j < nq, ATTN_Q_SCALE, 1.0)
        qkv_ref[...] = (_dot(hb_ref[...], w_ref[...]) * (_rstd(ss_ref, d) * scale)).astype(qkv_ref.dtype)

    @pl.when(j >= nqkv)
    def _():
        gates_ref[...] = jax.nn.sigmoid(_dot(hb_ref[...], w_ref[...]) * _rstd(ss_ref, d)).astype(gates_ref.dtype)


def _proj(hb, ss, w, layer, *, attn_w, ssm_w, d_model):
    t, k = hb.shape
    tm = _tile(t, 1024)
    tn = next(c for c in (1024, 512, 256, 128)
              if attn_w % c == 0 and ssm_w % c == 0 and (2 * d_model) % c == 0)
    nq, nqkv, nu, ng = attn_w // tn, 3 * attn_w // tn, ssm_w // tn, 2 * d_model // tn

    def wcol(j):
        return jnp.where(j < nqkv, j, j + nu)

    if w.ndim == 2:
        w_spec = pl.BlockSpec((k, tn), lambda i, j: (0, wcol(j)))
    else:
        w_spec = pl.BlockSpec((None, k, tn), lambda i, j: (layer, 0, wcol(j)))
    return pl.pallas_call(
        functools.partial(_proj_kernel, d=k, nq=nq, nqkv=nqkv),
        grid=(t // tm, nqkv + ng),
        in_specs=[pl.BlockSpec((tm, k), lambda i, j: (i, 0)),
                  w_spec,
                  pl.BlockSpec((tm, 1), lambda i, j: (i, 0))],
        out_specs=[pl.BlockSpec((tm, tn), lambda i, j: (i, jnp.minimum(j, nqkv - 1))),
                   pl.BlockSpec((tm, tn), lambda i, j: (i, jnp.maximum(j - nqkv, 0)))],
        out_shape=[jax.ShapeDtypeStruct((t, 3 * attn_w), BF16),
                   jax.ShapeDtypeStruct((t, 2 * d_model), BF16)],
        compiler_params=_params("parallel", "arbitrary"),
        name="proj",
    )(hb, w, ss)


def _proj_u_kernel(hb_ref, w_ref, ss_ref, o_ref, stage_ref, *, d, sub_len):
    acc = _dot(hb_ref[...], w_ref[...]) * _rstd(ss_ref, d)
    for c in range(acc.shape[1] // LANES):
        for r in range(SUBLANES):
            stage_ref[c, pl.ds(r, sub_len, stride=SUBLANES), :] = (
                acc[r * sub_len:(r + 1) * sub_len, c * LANES:(c + 1) * LANES])
    for c in range(acc.shape[1] // LANES):
        o_ref[:, c * LANES:(c + 1) * LANES] = stage_ref[c].astype(o_ref.dtype)


def _proj_u(hb, ss, w, layer, *, col0, width, batch, seq):
    t, k = hb.shape
    sub_len = seq // SUBLANES
    tn = _tile(width, 256)
    assert col0 % tn == 0
    joff = col0 // tn
    if w.ndim == 2:
        w_spec = pl.BlockSpec((k, tn), lambda b, j: (0, j + joff))
    else:
        w_spec = pl.BlockSpec((None, k, tn), lambda b, j: (layer, 0, j + joff))
    return pl.pallas_call(
        functools.partial(_proj_u_kernel, d=k, sub_len=sub_len),
        grid=(batch, width // tn),
        in_specs=[pl.BlockSpec((seq, k), lambda b, j: (b, 0)),
                  w_spec,
                  pl.BlockSpec((seq, 1), lambda b, j: (b, 0))],
        out_specs=pl.BlockSpec((seq, tn), lambda b, j: (b, j)),
        out_shape=jax.ShapeDtypeStruct((t, width), BF16),
        scratch_shapes=[pltpu.VMEM((tn // LANES, seq, LANES), F32)],
        compiler_params=_params("parallel", "parallel"),
        name="proj_u",
    )(hb, w, ss)


def _glu_kernel(y_ref, w_ref, o_ref, stage_ref, *, width):
    y = y_ref[...]
    res = y.astype(F32) * jax.nn.sigmoid(_dot(y, w_ref[...]))
    ntile = width // LANES
    sub_rows = res.shape[0] // SUBLANES
    for c in range(ntile):
        stage_ref[c] = res[:, c * LANES:(c + 1) * LANES]
    for r in range(SUBLANES):
        for c in range(ntile):
            lane0 = r * width + c * LANES
            o_ref[:, lane0:lane0 + LANES] = (
                stage_ref[c, pl.ds(r, sub_rows, stride=SUBLANES), :].astype(o_ref.dtype))


def _glu(y_perm, w, layer, *, batch, seq):
    t, width = y_perm.shape
    assert w.shape[-1] == width and w.shape[-2] == width
    tm = _tile(seq, 512)
    tiles = seq // tm
    return pl.pallas_call(
        functools.partial(_glu_kernel, width=width),
        grid=(t // tm,),
        in_specs=[pl.BlockSpec((tm, width), lambda i: (i, 0)),
                  (pl.BlockSpec((width, width), lambda i: (0, 0)) if w.ndim == 2
                   else pl.BlockSpec((None, width, width), lambda i: (layer, 0, 0)))],
        out_specs=pl.BlockSpec((None, tm // SUBLANES, SUBLANES * width), lambda i: (i // tiles, i % tiles, 0)),
        out_shape=jax.ShapeDtypeStruct((batch, seq // SUBLANES, SUBLANES * width), BF16),
        scratch_shapes=[pltpu.VMEM((width // LANES, tm, LANES), F32)],
        compiler_params=_params("parallel"),
        name="glu",
    )(y_perm, w)


def _merge_kernel(a_ref, wa_ref, s_ref, ws_ref, ga_ref, gs_ref, o_ref, *, nr, sub_len, width):
    up_a = _dot(a_ref[...], wa_ref[...])
    for rl in range(nr):
        rows = slice(rl * sub_len, (rl + 1) * sub_len)
        up_s = _dot(s_ref[:, rl * width:(rl + 1) * width], ws_ref[...])
        o_ref[rows, :] = (ga_ref[rows, :].astype(F32) * up_a[rows, :]
                          + gs_ref[rows, :].astype(F32) * up_s).astype(o_ref.dtype)


def _merge(attn, wa, ssm3, ws, gates, layer, *, seq):
    t, ka = attn.shape
    sub_len = seq // SUBLANES
    assert ssm3.shape[1] == sub_len
    ks = ssm3.shape[2] // SUBLANES
    d = wa.shape[-1]
    tm, tn = _tile(seq, 1024), _tile(d, 1024)
    assert tm % sub_len == 0
    nr = tm // sub_len
    tiles = seq // tm
    nj = d // tn
    return pl.pallas_call(
        functools.partial(_merge_kernel, nr=nr, sub_len=sub_len, width=ks),
        grid=(t // tm, nj),
        in_specs=[pl.BlockSpec((tm, ka), lambda i, j: (i, 0)),
                  _wspec(wa, layer, ka, tn),
                  pl.BlockSpec((None, sub_len, nr * ks), lambda i, j: (i // tiles, 0, i % tiles)),
                  _wspec(ws, layer, ks, tn),
                  pl.BlockSpec((tm, tn), lambda i, j: (i, j)),
                  pl.BlockSpec((tm, tn), lambda i, j: (i, j + nj))],
        out_specs=pl.BlockSpec((tm, tn), lambda i, j: (i, j)),
        out_shape=jax.ShapeDtypeStruct((t, d), BF16),
        compiler_params=_params("parallel", "parallel"),
        name="merge",
    )(attn, wa, ssm3, ws, gates, gates)


def _resid_mm_kernel(x_ref, w_ref, h_ref, *rest, nk, side):
    if side:
        sw_ref, sg_ref, o_ref, ob_ref, ss_ref, so_ref = rest
        _cast_block(sw_ref, sg_ref, so_ref)
    else:
        o_ref, ob_ref, ss_ref = rest
    j = pl.program_id(1)
    if nk == 1:
        _emit_stream(h_ref[...] + _dot(x_ref[...], w_ref[...]), o_ref, ob_ref, ss_ref, j)
        return
    k = pl.program_id(2)

    @pl.when(k == 0)
    def _():
        o_ref[...] = h_ref[...] + _dot(x_ref[...], w_ref[...])

    @pl.when(jnp.logical_and(k > 0, k < nk - 1))
    def _():
        o_ref[...] += _dot(x_ref[...], w_ref[...])

    @pl.when(k == nk - 1)
    def _():
        _emit_stream(o_ref[...] + _dot(x_ref[...], w_ref[...]), o_ref, ob_ref, ss_ref, j)


def _resid_matmul(x, w, layer, h, *, name, side=None):
    t, k = x.shape
    n = w.shape[-1]
    tk = k if k <= 4096 else _tile(k, 2048)
    nk = k // tk
    assert nk == 1 or nk >= 2
    tm, tn = _tile(t, 1024), _tile(n, 512 if nk == 1 else 1024)
    nj = n // tn
    in_specs = [pl.BlockSpec((tm, tk), lambda i, j, kk: (i, kk)),
                _wspec(w, layer, tk, tn),
                pl.BlockSpec((tm, tn), lambda i, j, kk: (i, j))]
    args = [x, w, h]
    out_specs, out_shape = _stream_out(t, n, tm, tn)
    if side is not None:
        s_in, s_out, s_shape = _side_cast_specs(side, (t // tm) * nj * nk,
                                                lambda i, j, kk: (i * nj + j) * nk + kk)
        in_specs += s_in
        args += [side.w, side.gain.reshape(*side.gain.shape, 1)]
        out_specs.append(s_out)
        out_shape.append(s_shape)
    return pl.pallas_call(
        functools.partial(_resid_mm_kernel, nk=nk, side=side is not None),
        grid=(t // tm, nj, nk),
        in_specs=in_specs,
        out_specs=out_specs,
        out_shape=out_shape,
        compiler_params=_params("parallel", "arbitrary", "arbitrary"),
        name=name,
    )(*args)


def _ple_kernel(hb_ref, wg_ref, ss_ref, p_ref, wp_ref, h_ref, o_ref, ob_ref, sso_ref, *, d):
    gate = jax.nn.sigmoid(_dot(hb_ref[...], wg_ref[...]) * _rstd(ss_ref, d))
    emb = _dot(p_ref[...].astype(BF16), wp_ref[...])
    _emit_stream(h_ref[...] + emb * gate, o_ref, ob_ref, sso_ref, pl.program_id(1))


def _ple(hb, ss, wg, p, wp, layer, h):
    t, k = hb.shape
    kp = p.shape[1]
    n = wg.shape[-1]
    tm, tn = _tile(t, 1024), _tile(n, 512)
    out_specs, out_shape = _stream_out(t, n, tm, tn)
    return pl.pallas_call(
        functools.partial(_ple_kernel, d=k),
        grid=(t // tm, n // tn),
        in_specs=[pl.BlockSpec((tm, k), lambda i, j: (i, 0)),
                  _wspec(wg, layer, k, tn),
                  pl.BlockSpec((tm, 1), lambda i, j: (i, 0)),
                  pl.BlockSpec((tm, kp), lambda i, j: (i, 0)),
                  _wspec(wp, layer, kp, tn),
                  pl.BlockSpec((tm, tn), lambda i, j: (i, j))],
        out_specs=out_specs,
        out_shape=out_shape,
        compiler_params=_params("parallel", "arbitrary"),
        name="ple",
    )(hb, wg, ss, p, wp, h)


def _attn_kernel(q_ref, k_ref, v_ref, tri_ref, *rest, tq, nheads, nsides):
    side_in, (o_ref, *side_out), (acc_ref, carry_ref) = rest[:2 * nsides], rest[2 * nsides:3 * nsides + 1], rest[3 * nsides + 1:]
    for n in range(nsides):
        _cast_block(side_in[2 * n], side_in[2 * n + 1], side_out[n])
    qi = pl.program_id(2)
    tri = tri_ref[...]
    row = lax.broadcasted_iota(jnp.int32, (tq, tq), 0)
    col = lax.broadcasted_iota(jnp.int32, (tq, tq), 1)
    causal = col < row

    def key_block(kb, masked):
        start = pl.multiple_of(kb * tq, tq)
        heads = range(nheads)
        lanes = [slice(hd * HEAD_DIM, (hd + 1) * HEAD_DIM) for hd in heads]
        zs = [lax.dot_general(q_ref[:, lanes[hd]], k_ref[pl.ds(start, tq), lanes[hd]],
                              (((1,), (1,)), ((), ())), preferred_element_type=F32) for hd in heads]
        tails = []
        for z in zs:
            neg_abs = lax.bitcast_convert_type(
                lax.bitcast_convert_type(z, jnp.uint32) | jnp.uint32(0x80000000), F32)
            softplus = jnp.maximum(z, 0.0) + jnp.log2(1.0 + jnp.exp2(neg_abs))
            if masked:
                softplus = jnp.where(causal, softplus, 0.0)
            hi = softplus.astype(BF16)
            lo = (softplus - hi.astype(F32)).astype(BF16)
            tails.append(_dot(jnp.concatenate([hi, lo], axis=1), tri))
        for hd in heads:
            w = jnp.exp2(zs[hd] + tails[hd] + carry_ref[hd])
            if masked:
                w = jnp.where(causal, w, 0.0)
            acc_ref[:, lanes[hd]] += _dot(w.astype(BF16), v_ref[pl.ds(start, tq), lanes[hd]])
            carry_ref[hd] += tails[hd][:, 0:1]

    acc_ref[...] = jnp.zeros_like(acc_ref)
    carry_ref[...] = jnp.zeros_like(carry_ref)
    key_block(qi, True)

    def body(i, c):
        key_block(qi - 1 - i, False)
        return c

    lax.fori_loop(0, qi, body, 0)
    o_ref[...] = acc_ref[...].astype(o_ref.dtype)


def _attention(qkv, sides, *, batch, seq, heads):
    tq = _tile(seq, 256)
    nq = seq // tq
    nheads = _tile(heads, ATTN_HEADS_PER_STEP)
    hgroups = heads // nheads
    width = nheads * HEAD_DIM
    r = jnp.arange(tq)
    tri = -(r[:, None] >= r[None, :]).astype(BF16)
    tri2 = jnp.concatenate([tri, tri], axis=0)
    nsteps = batch * hgroups * nq
    s_in, s_args, s_out, s_shape = [], [], [], []
    for side in sides:
        specs, out_spec, out_shape = _side_cast_specs(side, nsteps, lambda b, h, i: (b * hgroups + h) * nq + i)
        s_in += specs
        s_args += [side.w, side.gain.reshape(*side.gain.shape, 1)]
        s_out.append(out_spec)
        s_shape.append(out_shape)
    out, *cast = pl.pallas_call(
        functools.partial(_attn_kernel, tq=tq, nheads=nheads, nsides=len(sides)),
        grid=(batch, hgroups, nq),
        in_specs=[pl.BlockSpec((tq, width), lambda b, h, i: (b * nq + i, h)),
                  pl.BlockSpec((seq, width), lambda b, h, i: (b, hgroups + h)),
                  pl.BlockSpec((seq, width), lambda b, h, i: (b, 2 * hgroups + h)),
                  pl.BlockSpec((2 * tq, tq), lambda b, h, i: (0, 0))] + s_in,
        out_specs=[pl.BlockSpec((tq, width), lambda b, h, i: (b * nq + i, h))] + s_out,
        out_shape=[jax.ShapeDtypeStruct((batch * seq, heads * HEAD_DIM), BF16)] + s_shape,
        scratch_shapes=[pltpu.VMEM((tq, width), F32), pltpu.VMEM((nheads, tq, 1), F32)],
        compiler_params=_params("arbitrary", "arbitrary", "arbitrary"),
        name="stickbreak_attn",
    )(qkv, qkv, qkv, tri2, *s_args)
    return out, cast


def _ssm_kernel(u_ref, b_ref, m_ref, c_ref, d_ref, o_ref, x_ref, xb_ref, *, seq):
    ns = SLAB_STATES
    nchunks = seq // SSM_CHUNK
    steps = SSM_CHUNK // SUBLANES

    def rows(k):
        return slice(k * SSM_CHUNK, (k + 1) * SSM_CHUNK)

    def group(i):
        return slice(i * SUBLANES, (i + 1) * SUBLANES)

    def project(k):
        x_ref[rows(k), :] = _dot(u_ref[rows(k), :], b_ref[0])

    def scan(k, v):
        v_re, v_im = v
        for s in range(steps):
            r = group(k * steps + s)
            l_re = m_ref[0, 0]
            l_im = m_ref[0, 1]
            v_re, v_im = (l_re * v_re - l_im * v_im + x_ref[r, 0:ns],
                          l_re * v_im + l_im * v_re + x_ref[r, ns:2 * ns])
            x_ref[r, 0:ns] = v_re
            x_ref[r, ns:2 * ns] = v_im
        return v_re, v_im

    def carry_in(e):
        e_re, e_im = e
        for n, shift in enumerate((1, 2, 4)):
            a_re = m_ref[0, 2 + 2 * n]
            a_im = m_ref[0, 3 + 2 * n]
            s_re = pltpu.roll(e_re, shift, 0)
            s_im = pltpu.roll(e_im, shift, 0)
            e_re, e_im = (e_re + (a_re * s_re - a_im * s_im),
                          e_im + (a_re * s_im + a_im * s_re))
        not_first = m_ref[0, 8]
        return not_first * pltpu.roll(e_re, 1, 0), not_first * pltpu.roll(e_im, 1, 0)

    def fix(k, w):
        w_re, w_im = w
        for s in range(0, steps, 2):
            parts = []
            for i in (k * steps + s, k * steps + s + 1):
                l_re = m_ref[0, 0]
                l_im = m_ref[0, 1]
                w_re, w_im = l_re * w_re - l_im * w_im, l_re * w_im + l_im * w_re
                parts.append(jnp.concatenate([x_ref[group(i), 0:ns] + w_re,
                                              x_ref[group(i), ns:2 * ns] + w_im], axis=1))
            i0 = k * steps + s
            xb_ref[i0 * SUBLANES:(i0 + 2) * SUBLANES, :] = jnp.concatenate(parts, axis=0).astype(BF16)
        return w_re, w_im

    def readout(k):
        y = _dot(xb_ref[rows(k), :], c_ref[0]) + d_ref[...] * u_ref[rows(k), :].astype(F32)
        o_ref[rows(k), :] = jax.nn.gelu(y, approximate=True).astype(o_ref.dtype)

    project(0)
    if nchunks > 1:
        project(1)
    zero = jnp.zeros((SUBLANES, ns), F32)
    v = scan(0, (zero, zero))
    for k in range(1, nchunks):
        if k + 1 < nchunks:
            project(k + 1)
        v = scan(k, v)
    w = fix(0, carry_in(v))
    for k in range(1, nchunks):
        w = fix(k, w)
        readout(k - 1)
    readout(nchunks - 1)


def _ssm_operands(lam_re, lam_im, log_dt, b_re, b_im, c_re, c_im, sub_len):
    g = lam_re.shape[0]
    nslab = g // SLAB_GROUPS
    dt = jnp.exp(log_dt)[:, None]
    lam = lax.complex(lam_re, lam_im)
    lam_bar = jnp.exp(lam * dt)
    b_bar = ((lam_bar - 1.0) / lam)[..., None] * lax.complex(b_re, b_im)
    eye = jnp.eye(SLAB_GROUPS, dtype=F32)

    def b_block(part):
        t = part.reshape(nslab, SLAB_GROUPS, SSM_STATE, SSM_GROUP).transpose(0, 1, 3, 2)
        return jnp.einsum("sgcp,gh->sgchp", t, eye).reshape(nslab, SLAB, SLAB_STATES)

    def c_block(part):
        t = part.reshape(nslab, SLAB_GROUPS, SSM_GROUP, SSM_STATE).transpose(0, 1, 3, 2)
        return jnp.einsum("sgpc,gh->sgphc", t, eye).reshape(nslab, SLAB_STATES, SLAB)

    b_mat = jnp.concatenate([b_block(b_bar.real), b_block(b_bar.imag)], axis=2).astype(BF16)
    c_mat = jnp.concatenate([c_block(c_re), c_block(-c_im)], axis=1).astype(BF16)

    lam1 = lam_bar.reshape(nslab, 1, SLAB_STATES)
    big1, sq, n = jnp.ones_like(lam1), lam1, sub_len
    while n:
        if n & 1:
            big1 = big1 * sq
        sq, n = sq * sq, n >> 1
    big2 = big1 * big1
    big4 = big2 * big2
    rows = jnp.arange(SUBLANES)[None, :, None]
    every = jnp.broadcast_to(lam1, (nslab, SUBLANES, SLAB_STATES))
    planes = [every.real, every.imag]
    for big, shift in ((big1, 1), (big2, 2), (big4, 4)):
        a = jnp.where(rows >= shift, big, 0.0)
        planes += [a.real, a.imag]
    planes.append(jnp.broadcast_to((rows >= 1).astype(F32), (nslab, SUBLANES, SLAB_STATES)))
    mult = jnp.stack(planes, axis=1).astype(F32)
    return b_mat, mult, c_mat


def _ssm(u_perm, b_mat, mult, c_mat, d_skip, layer, *, batch, seq):
    nslab = b_mat.shape[1]
    width = nslab * SLAB
    sub_len = seq // SUBLANES
    assert seq % SSM_CHUNK == 0 and (SSM_CHUNK // SUBLANES) % 2 == 0
    return pl.pallas_call(
        functools.partial(_ssm_kernel, seq=seq),
        grid=(batch, nslab),
        in_specs=[pl.BlockSpec((seq, SLAB), lambda b, s: (b, s)),
                  pl.BlockSpec((None, 1, SLAB, 2 * SLAB_STATES), lambda b, s: (layer, s, 0, 0)),
                  pl.BlockSpec((None, 1, 9, SUBLANES, SLAB_STATES), lambda b, s: (layer, s, 0, 0, 0)),
                  pl.BlockSpec((None, 1, 2 * SLAB_STATES, SLAB), lambda b, s: (layer, s, 0, 0)),
                  pl.BlockSpec((None, 1, SLAB), lambda b, s: (layer, 0, s))],
        out_specs=pl.BlockSpec((seq, SLAB), lambda b, s: (b, s)),
        out_shape=jax.ShapeDtypeStruct((batch * seq, width), BF16),
        scratch_shapes=[pltpu.VMEM((seq, 2 * SLAB_STATES), F32), pltpu.VMEM((seq, 2 * SLAB_STATES), BF16)],
        compiler_params=_params("parallel", "parallel"),
        name="s5_ssm",
    )(u_perm, b_mat, mult, c_mat, d_skip.reshape(-1, 1, width))


def kernel(x, p, g_mix, w_in, w_br_attn, lam_re, lam_im, log_dt, b_re, b_im, c_re, c_im, d_skip, w_glu, w_br_ssm, w_o, g_mlp, w_ff1, w_ff2, g_ple, w_ple_gate, w_ple, g_final):
    bsz, seq, d = x.shape
    depth = w_in.shape[0]
    attn_w = w_br_attn.shape[1]
    ssm_w = w_br_ssm.shape[1]
    heads = attn_w // HEAD_DIM
    t = bsz * seq
    assert w_in.shape[2] == 3 * attn_w + ssm_w + 2 * d

    wb_in = _to_bf16(w_in, g_mix, layer=0)
    wb_ple = _to_bf16(w_ple)
    b_mat, mult, c_mat = jax.vmap(functools.partial(_ssm_operands, sub_len=seq // SUBLANES))(
        lam_re, lam_im, log_dt, b_re, b_im, c_re, c_im)

    def plain(w):
        return jnp.ones(w.shape[:2], F32)

    h = x.reshape(t, d)
    hb, ss = _enter(h)
    for i in range(depth):
        qkv, gates = _proj(hb, ss, wb_in, i, attn_w=attn_w, ssm_w=ssm_w, d_model=d)
        u_perm = _proj_u(hb, ss, wb_in, i, col0=3 * attn_w, width=ssm_w, batch=bsz, seq=seq)
        attn, (wb_ff1, wb_o, wb_ple_gate, wb_br_attn, wb_br_ssm, wb_glu) = _attention(
            qkv,
            [_SideCast(w_ff1, g_mlp, i), _SideCast(w_o, plain(w_o), i), _SideCast(w_ple_gate, g_ple, i),
             _SideCast(w_br_attn, plain(w_br_attn), i), _SideCast(w_br_ssm, plain(w_br_ssm), i),
             _SideCast(w_glu, plain(w_glu), i)],
            batch=bsz, seq=seq, heads=heads)
        y_perm = _ssm(u_perm, b_mat, mult, c_mat, d_skip, i, batch=bsz, seq=seq)
        ssm3 = _glu(y_perm, wb_glu, i, batch=bsz, seq=seq)
        merged = _merge(attn, wb_br_attn, ssm3, wb_br_ssm, gates, i, seq=seq)
        h, hb, ss = _resid_matmul(merged, wb_o, i, h, name="out_proj")

        act, wb_ff2 = _normed_relu2_matmul(hb, ss, wb_ff1, i, name="ff1",
                                           side=_SideCast(w_ff2, plain(w_ff2), i))
        if i + 1 < depth:
            h, hb, ss, wb_in = _resid_matmul(act, wb_ff2, i, h, name="ff2",
                                             side=_SideCast(w_in, g_mix, i + 1))
        else:
            h, hb, ss = _resid_matmul(act, wb_ff2, i, h, name="ff2")

        h, hb, ss = _ple(hb, ss, wb_ple_gate, p[i].reshape(t, -1), wb_ple, i, h)
    return _rmsnorm(h, g_final, x.dtype).reshape(bsz, seq, d)
```

```python
import functools
import math
from typing import NamedTuple

import jax
import jax.numpy as jnp
from jax import lax
from jax.experimental import pallas as pl
from jax.experimental.pallas import tpu as pltpu

F32 = jnp.float32
BF16 = jnp.bfloat16

RMS_EPS = 1e-6
HEAD_DIM = 128
ATTN_Q_SCALE = HEAD_DIM ** -0.5 * math.log2(math.e)
ATTN_HEADS_PER_STEP = 8
SSM_GROUP = 16
SSM_STATE = 64
SLAB = 256
SLAB_GROUPS = SLAB // SSM_GROUP
SLAB_STATES = SLAB_GROUPS * SSM_STATE
SUBLANES = 8
LANES = 128
SSM_CHUNK = 512
V7X_VMEM_LIMIT = 60 * 1024 * 1024
CAST_BLOCK_BYTES = 8 * 1024 * 1024


def _tile(dim, pref):
    if dim % pref == 0:
        return pref
    assert dim < pref, (dim, pref)
    return dim


def _params(*sem):
    return pltpu.CompilerParams(dimension_semantics=sem, vmem_limit_bytes=V7X_VMEM_LIMIT)


def _dot(a, b):
    return jnp.dot(a, b, preferred_element_type=F32)


def _wspec(w, layer, k, tn, joff=0):
    if w.ndim == 2:
        return pl.BlockSpec((k, tn), lambda i, j, *kk: (kk[0] if kk else 0, j + joff))
    return pl.BlockSpec((None, k, tn), lambda i, j, *kk: (layer, kk[0] if kk else 0, j + joff))


class _SideCast(NamedTuple):
    w: jax.Array
    gain: jax.Array
    layer: int


def _side_cast_specs(side, nsteps, step_of):
    depth, k, n = side.w.shape
    nblk = max(b for b in range(1, nsteps + 1) if k % b == 0 and (k // b) % 16 == 0)
    bk = k // nblk

    def blk(*g):
        return jnp.minimum(step_of(*g), nblk - 1)

    in_specs = [pl.BlockSpec((None, bk, n), lambda *g: (side.layer, blk(*g), 0)),
                pl.BlockSpec((None, bk, 1), lambda *g: (side.layer, blk(*g), 0))]
    return in_specs, pl.BlockSpec((bk, n), lambda *g: (blk(*g), 0)), jax.ShapeDtypeStruct((k, n), BF16)


def _cast_block(w_ref, g_ref, o_ref):
    o_ref[...] = (w_ref[...] * g_ref[...]).astype(o_ref.dtype)


def _rstd(ss_ref, d):
    return lax.rsqrt(ss_ref[...] * (1.0 / d) + RMS_EPS)


def _emit_stream(h, h_ref, hb_ref, ss_ref, j):
    h_ref[...] = h
    hb_ref[...] = h.astype(BF16)
    row_ss = jnp.sum(h * h, axis=1, keepdims=True)

    @pl.when(j == 0)
    def _():
        ss_ref[...] = row_ss

    @pl.when(j > 0)
    def _():
        ss_ref[...] += row_ss


def _stream_out(t, n, tm, tn):
    specs = [pl.BlockSpec((tm, tn), lambda i, j, *kk: (i, j)),
             pl.BlockSpec((tm, tn), lambda i, j, *kk: (i, j)),
             pl.BlockSpec((tm, 1), lambda i, j, *kk: (i, 0))]
    shapes = [jax.ShapeDtypeStruct((t, n), F32), jax.ShapeDtypeStruct((t, n), BF16),
              jax.ShapeDtypeStruct((t, 1), F32)]
    return specs, shapes


def _to_bf16(w, gain=None, layer=None):
    depth, k, n = w.shape
    bk = max(16, min(k, CAST_BLOCK_BYTES // (4 * n)))
    assert k % bk == 0, (k, bk)
    if gain is None:
        gain = jnp.ones((depth, k), F32)
    if layer is None:
        grid, first = (depth, k // bk), 0
        out_spec = pl.BlockSpec((None, bk, n), lambda a, r: (a, r, 0))
        out_shape = jax.ShapeDtypeStruct(w.shape, BF16)
    else:
        grid, first = (1, k // bk), layer
        out_spec = pl.BlockSpec((bk, n), lambda a, r: (r, 0))
        out_shape = jax.ShapeDtypeStruct((k, n), BF16)
    return pl.pallas_call(
        _cast_block,
        grid=grid,
        in_specs=[pl.BlockSpec((None, bk, n), lambda a, r: (a + first, r, 0)),
                  pl.BlockSpec((None, bk, 1), lambda a, r: (a + first, r, 0))],
        out_specs=out_spec,
        out_shape=out_shape,
        compiler_params=_params("parallel", "parallel"),
        name="cast_bf16",
    )(w, gain.reshape(depth, k, 1))


def _enter_kernel(x_ref, xb_ref, ss_ref):
    x = x_ref[...]
    xb_ref[...] = x.astype(BF16)
    ss_ref[...] = jnp.sum(x * x, axis=1, keepdims=True)


def _enter(x):
    t, d = x.shape
    tm = _tile(t, 256)
    return pl.pallas_call(
        _enter_kernel,
        grid=(t // tm,),
        in_specs=[pl.BlockSpec((tm, d), lambda i: (i, 0))],
        out_specs=[pl.BlockSpec((tm, d), lambda i: (i, 0)), pl.BlockSpec((tm, 1), lambda i: (i, 0))],
        out_shape=[jax.ShapeDtypeStruct((t, d), BF16), jax.ShapeDtypeStruct((t, 1), F32)],
        compiler_params=_params("parallel"),
        name="enter",
    )(x)


def _rmsnorm_kernel(x_ref, g_ref, o_ref):
    x = x_ref[...]
    ms = jnp.mean(x * x, axis=-1, keepdims=True)
    o_ref[...] = (x * lax.rsqrt(ms + RMS_EPS) * g_ref[...]).astype(o_ref.dtype)


def _rmsnorm(x, g, out_dtype):
    t, d = x.shape
    tm = _tile(t, 256)
    return pl.pallas_call(
        _rmsnorm_kernel,
        grid=(t // tm,),
        in_specs=[pl.BlockSpec((tm, d), lambda i: (i, 0)),
                  pl.BlockSpec((1, d), lambda i: (0, 0))],
        out_specs=pl.BlockSpec((tm, d), lambda i: (i, 0)),
        out_shape=jax.ShapeDtypeStruct((t, d), out_dtype),
        compiler_params=_params("parallel"),
        name="rmsnorm",
    )(x, g.reshape(1, d))


def _normed_mm_kernel(hb_ref, w_ref, ss_ref, *rest, d, side):
    if side:
        sw_ref, sg_ref, o_ref, so_ref = rest
        _cast_block(sw_ref, sg_ref, so_ref)
    else:
        (o_ref,) = rest
    acc = _dot(hb_ref[...], w_ref[...]) * _rstd(ss_ref, d)
    o_ref[...] = jnp.square(jnp.maximum(acc, 0.0)).astype(o_ref.dtype)


def _normed_relu2_matmul(hb, ss, w, layer, *, name, side=None):
    t, k = hb.shape
    ncols = w.shape[-1]
    tm, tn = _tile(t, 1024), _tile(ncols, 1024)
    nj = ncols // tn
    in_specs = [pl.BlockSpec((tm, k), lambda i, j: (i, 0)),
                _wspec(w, layer, k, tn),
                pl.BlockSpec((tm, 1), lambda i, j: (i, 0))]
    args = [hb, w, ss]
    out_specs = [pl.BlockSpec((tm, tn), lambda i, j: (i, j))]
    out_shape = [jax.ShapeDtypeStruct((t, ncols), BF16)]
    if side is not None:
        s_in, s_out, s_shape = _side_cast_specs(side, (t // tm) * nj, lambda i, j: i * nj + j)
        in_specs += s_in
        args += [side.w, side.gain.reshape(*side.gain.shape, 1)]
        out_specs.append(s_out)
        out_shape.append(s_shape)
    outs = pl.pallas_call(
        functools.partial(_normed_mm_kernel, d=k, side=side is not None),
        grid=(t // tm, nj),
        in_specs=in_specs,
        out_specs=out_specs,
        out_shape=out_shape,
        compiler_params=_params("parallel", "parallel"),
        name=name,
    )(*args)
    return outs if side is not None else outs[0]


def _proj_kernel(hb_ref, w_ref, ss_ref, qkv_ref, gates_ref, *, d, nq, nqkv):
    j = pl.program_id(1)

    @pl.when(j < nqkv)
    def _():
        scale = jnp.where(j < nq, ATTN_Q_SCALE, 1.0)
        qkv_ref[...] = (_dot(hb_ref[...], w_ref[...]) * (_rstd(ss_ref, d) * scale)).astype(qkv_ref.dtype)

    @pl.when(j >= nqkv)
    def _():
        gates_ref[...] = jax.nn.sigmoid(_dot(hb_ref[...], w_ref[...]) * _rstd(ss_ref, d)).astype(gates_ref.dtype)


def _proj(hb, ss, w, layer, *, attn_w, ssm_w, d_model):
    t, k = hb.shape
    tm = _tile(t, 1024)
    tn = next(c for c in (1024, 512, 256, 128)
              if attn_w % c == 0 and ssm_w % c == 0 and (2 * d_model) % c == 0)
    nq, nqkv, nu, ng = attn_w // tn, 3 * attn_w // tn, ssm_w // tn, 2 * d_model // tn

    def wcol(j):
        return jnp.where(j < nqkv, j, j + nu)

    if w.ndim == 2:
        w_spec = pl.BlockSpec((k, tn), lambda i, j: (0, wcol(j)))
    else:
        w_spec = pl.BlockSpec((None, k, tn), lambda i, j: (layer, 0, wcol(j)))
    return pl.pallas_call(
        functools.partial(_proj_kernel, d=k, nq=nq, nqkv=nqkv),
        grid=(t // tm, nqkv + ng),
        in_specs=[pl.BlockSpec((tm, k), lambda i, j: (i, 0)),
                  w_spec,
                  pl.BlockSpec((tm, 1), lambda i, j: (i, 0))],
        out_specs=[pl.BlockSpec((tm, tn), lambda i, j: (i, jnp.minimum(j, nqkv - 1))),
                   pl.BlockSpec((tm, tn), lambda i, j: (i, jnp.maximum(j - nqkv, 0)))],
        out_shape=[jax.ShapeDtypeStruct((t, 3 * attn_w), BF16),
                   jax.ShapeDtypeStruct((t, 2 * d_model), BF16)],
        compiler_params=_params("parallel", "arbitrary"),
        name="proj",
    )(hb, w, ss)


def _proj_u_kernel(hb_ref, w_ref, ss_ref, o_ref, stage_ref, *, d, sub_len):
    acc = _dot(hb_ref[...], w_ref[...]) * _rstd(ss_ref, d)
    for c in range(acc.shape[1] // LANES):
        for r in range(SUBLANES):
            stage_ref[c, pl.ds(r, sub_len, stride=SUBLANES), :] = (
                acc[r * sub_len:(r + 1) * sub_len, c * LANES:(c + 1) * LANES])
    for c in range(acc.shape[1] // LANES):
        o_ref[:, c * LANES:(c + 1) * LANES] = stage_ref[c].astype(o_ref.dtype)


def _proj_u(hb, ss, w, layer, *, col0, width, batch, seq):
    t, k = hb.shape
    sub_len = seq // SUBLANES
    tn = _tile(width, 256)
    assert col0 % tn == 0
    joff = col0 // tn
    if w.ndim == 2:
        w_spec = pl.BlockSpec((k, tn), lambda b, j: (0, j + joff))
    else:
        w_spec = pl.BlockSpec((None, k, tn), lambda b, j: (layer, 0, j + joff))
    return pl.pallas_call(
        functools.partial(_proj_u_kernel, d=k, sub_len=sub_len),
        grid=(batch, width // tn),
        in_specs=[pl.BlockSpec((seq, k), lambda b, j: (b, 0)),
                  w_spec,
                  pl.BlockSpec((seq, 1), lambda b, j: (b, 0))],
        out_specs=pl.BlockSpec((seq, tn), lambda b, j: (b, j)),
        out_shape=jax.ShapeDtypeStruct((t, width), BF16),
        scratch_shapes=[pltpu.VMEM((tn // LANES, seq, LANES), F32)],
        compiler_params=_params("parallel", "parallel"),
        name="proj_u",
    )(hb, w, ss)


def _glu_kernel(y_ref, w_ref, o_ref, stage_ref, *, width):
    y = y_ref[...]
    res = y.astype(F32) * jax.nn.sigmoid(_dot(y, w_ref[...]))
    ntile = width // LANES
    sub_rows = res.shape[0] // SUBLANES
    for c in range(ntile):
        stage_ref[c] = res[:, c * LANES:(c + 1) * LANES]
    for r in range(SUBLANES):
        for c in range(ntile):
            lane0 = r * width + c * LANES
            o_ref[:, lane0:lane0 + LANES] = (
                stage_ref[c, pl.ds(r, sub_rows, stride=SUBLANES), :].astype(o_ref.dtype))


def _glu(y_perm, w, layer, *, batch, seq):
    t, width = y_perm.shape
    assert w.shape[-1] == width and w.shape[-2] == width
    tm = _tile(seq, 512)
    tiles = seq // tm
    return pl.pallas_call(
        functools.partial(_glu_kernel, width=width),
        grid=(t // tm,),
        in_specs=[pl.BlockSpec((tm, width), lambda i: (i, 0)),
                  (pl.BlockSpec((width, width), lambda i: (0, 0)) if w.ndim == 2
                   else pl.BlockSpec((None, width, width), lambda i: (layer, 0, 0)))],
        out_specs=pl.BlockSpec((None, tm // SUBLANES, SUBLANES * width), lambda i: (i // tiles, i % tiles, 0)),
        out_shape=jax.ShapeDtypeStruct((batch, seq // SUBLANES, SUBLANES * width), BF16),
        scratch_shapes=[pltpu.VMEM((width // LANES, tm, LANES), F32)],
        compiler_params=_params("parallel"),
        name="glu",
    )(y_perm, w)


def _merge_kernel(a_ref, wa_ref, s_ref, ws_ref, ga_ref, gs_ref, o_ref, *, nr, sub_len, width):
    up_a = _dot(a_ref[...], wa_ref[...])
    for rl in range(nr):
        rows = slice(rl * sub_len, (rl + 1) * sub_len)
        up_s = _dot(s_ref[:, rl * width:(rl + 1) * width], ws_ref[...])
        o_ref[rows, :] = (ga_ref[rows, :].astype(F32) * up_a[rows, :]
                          + gs_ref[rows, :].astype(F32) * up_s).astype(o_ref.dtype)


def _merge(attn, wa, ssm3, ws, gates, layer, *, seq):
    t, ka = attn.shape
    sub_len = seq // SUBLANES
    assert ssm3.shape[1] == sub_len
    ks = ssm3.shape[2] // SUBLANES
    d = wa.shape[-1]
    tm, tn = _tile(seq, 1024), _tile(d, 1024)
    assert tm % sub_len == 0
    nr = tm // sub_len
    tiles = seq // tm
    nj = d // tn
    return pl.pallas_call(
        functools.partial(_merge_kernel, nr=nr, sub_len=sub_len, width=ks),
        grid=(t // tm, nj),
        in_specs=[pl.BlockSpec((tm, ka), lambda i, j: (i, 0)),
                  _wspec(wa, layer, ka, tn),
                  pl.BlockSpec((None, sub_len, nr * ks), lambda i, j: (i // tiles, 0, i % tiles)),
                  _wspec(ws, layer, ks, tn),
                  pl.BlockSpec((tm, tn), lambda i, j: (i, j)),
                  pl.BlockSpec((tm, tn), lambda i, j: (i, j + nj))],
        out_specs=pl.BlockSpec((tm, tn), lambda i, j: (i, j)),
        out_shape=jax.ShapeDtypeStruct((t, d), BF16),
        compiler_params=_params("parallel", "parallel"),
        name="merge",
    )(attn, wa, ssm3, ws, gates, gates)


def _resid_mm_kernel(x_ref, w_ref, h_ref, *rest, nk, side):
    if side:
        sw_ref, sg_ref, o_ref, ob_ref, ss_ref, so_ref = rest
        _cast_block(sw_ref, sg_ref, so_ref)
    else:
        o_ref, ob_ref, ss_ref = rest
    j = pl.program_id(1)
    if nk == 1:
        _emit_stream(h_ref[...] + _dot(x_ref[...], w_ref[...]), o_ref, ob_ref, ss_ref, j)
        return
    k = pl.program_id(2)

    @pl.when(k == 0)
    def _():
        o_ref[...] = h_ref[...] + _dot(x_ref[...], w_ref[...])

    @pl.when(jnp.logical_and(k > 0, k < nk - 1))
    def _():
        o_ref[...] += _dot(x_ref[...], w_ref[...])

    @pl.when(k == nk - 1)
    def _():
        _emit_stream(o_ref[...] + _dot(x_ref[...], w_ref[...]), o_ref, ob_ref, ss_ref, j)


def _resid_matmul(x, w, layer, h, *, name, side=None):
    t, k = x.shape
    n = w.shape[-1]
    tk = k if k <= 4096 else _tile(k, 2048)
    nk = k // tk
    assert nk == 1 or nk >= 2
    tm, tn = _tile(t, 1024), _tile(n, 1024)
    nj = n // tn
    x_mode = dict(pipeline_mode=pl.Buffered(1)) if nk == 1 else {}
    in_specs = [pl.BlockSpec((tm, tk), lambda i, j, kk: (i, kk), **x_mode),
                _wspec(w, layer, tk, tn),
                pl.BlockSpec((tm, tn), lambda i, j, kk: (i, j))]
    args = [x, w, h]
    out_specs, out_shape = _stream_out(t, n, tm, tn)
    if side is not None:
        s_in, s_out, s_shape = _side_cast_specs(side, (t // tm) * nj * nk,
                                                lambda i, j, kk: (i * nj + j) * nk + kk)
        in_specs += s_in
        args += [side.w, side.gain.reshape(*side.gain.shape, 1)]
        out_specs.append(s_out)
        out_shape.append(s_shape)
    return pl.pallas_call(
        functools.partial(_resid_mm_kernel, nk=nk, side=side is not None),
        grid=(t // tm, nj, nk),
        in_specs=in_specs,
        out_specs=out_specs,
        out_shape=out_shape,
        compiler_params=_params("parallel", "arbitrary", "arbitrary"),
        name=name,
    )(*args)


def _ple_kernel(hb_ref, wg_ref, ss_ref, p_ref, wp_ref, h_ref, o_ref, ob_ref, sso_ref, *, d):
    gate = jax.nn.sigmoid(_dot(hb_ref[...], wg_ref[...]) * _rstd(ss_ref, d))
    emb = _dot(p_ref[...].astype(BF16), wp_ref[...])
    _emit_stream(h_ref[...] + emb * gate, o_ref, ob_ref, sso_ref, pl.program_id(1))


def _ple(hb, ss, wg, p, wp, layer, h):
    t, k = hb.shape
    kp = p.shape[1]
    n = wg.shape[-1]
    tm, tn = _tile(t, 1024), _tile(n, 1024)
    out_specs, out_shape = _stream_out(t, n, tm, tn)
    once = dict(pipeline_mode=pl.Buffered(1))
    return pl.pallas_call(
        functools.partial(_ple_kernel, d=k),
        grid=(t // tm, n // tn),
        in_specs=[pl.BlockSpec((tm, k), lambda i, j: (i, 0), **once),
                  _wspec(wg, layer, k, tn),
                  pl.BlockSpec((tm, 1), lambda i, j: (i, 0)),
                  pl.BlockSpec((tm, kp), lambda i, j: (i, 0), **once),
                  _wspec(wp, layer, kp, tn),
                  pl.BlockSpec((tm, tn), lambda i, j: (i, j))],
        out_specs=out_specs,
        out_shape=out_shape,
        compiler_params=_params("parallel", "arbitrary"),
        name="ple",
    )(hb, wg, ss, p, wp, h)


def _attn_kernel(q_ref, k_ref, v_ref, tri_ref, *rest, tq, nheads, nsides):
    side_in, (o_ref, *side_out), (acc_ref, carry_ref) = rest[:2 * nsides], rest[2 * nsides:3 * nsides + 1], rest[3 * nsides + 1:]
    for n in range(nsides):
        _cast_block(side_in[2 * n], side_in[2 * n + 1], side_out[n])
    qi = pl.program_id(2)
    tri = tri_ref[...]
    row = lax.broadcasted_iota(jnp.int32, (tq, tq), 0)
    col = lax.broadcasted_iota(jnp.int32, (tq, tq), 1)
    causal = col < row

    def key_block(kb, masked):
        start = pl.multiple_of(kb * tq, tq)
        heads = range(nheads)
        lanes = [slice(hd * HEAD_DIM, (hd + 1) * HEAD_DIM) for hd in heads]
        zs = [lax.dot_general(q_ref[:, lanes[hd]], k_ref[pl.ds(start, tq), lanes[hd]],
                              (((1,), (1,)), ((), ())), preferred_element_type=F32) for hd in heads]
        tails = []
        for z in zs:
            neg_abs = lax.bitcast_convert_type(
                lax.bitcast_convert_type(z, jnp.uint32) | jnp.uint32(0x80000000), F32)
            softplus = jnp.maximum(z, 0.0) + jnp.log2(1.0 + jnp.exp2(neg_abs))
            if masked:
                softplus = jnp.where(causal, softplus, 0.0)
            hi = softplus.astype(BF16)
            lo = (softplus - hi.astype(F32)).astype(BF16)
            tails.append(_dot(jnp.concatenate([hi, lo], axis=1), tri))
        for hd in heads:
            w = jnp.exp2(zs[hd] + tails[hd] + carry_ref[hd])
            if masked:
                w = jnp.where(causal, w, 0.0)
            acc_ref[:, lanes[hd]] += _dot(w.astype(BF16), v_ref[pl.ds(start, tq), lanes[hd]])
            carry_ref[hd] += tails[hd][:, 0:1]

    acc_ref[...] = jnp.zeros_like(acc_ref)
    carry_ref[...] = jnp.zeros_like(carry_ref)
    key_block(qi, True)

    def body(i, c):
        key_block(qi - 1 - i, False)
        return c

    lax.fori_loop(0, qi, body, 0)
    o_ref[...] = acc_ref[...].astype(o_ref.dtype)


def _attention(qkv, sides, *, batch, seq, heads):
    tq = _tile(seq, 256)
    nq = seq // tq
    nheads = _tile(heads, ATTN_HEADS_PER_STEP)
    hgroups = heads // nheads
    width = nheads * HEAD_DIM
    r = jnp.arange(tq)
    tri = -(r[:, None] >= r[None, :]).astype(BF16)
    tri2 = jnp.concatenate([tri, tri], axis=0)
    nsteps = batch * hgroups * nq
    s_in, s_args, s_out, s_shape = [], [], [], []
    for side in sides:
        specs, out_spec, out_shape = _side_cast_specs(side, nsteps, lambda b, h, i: (b * hgroups + h) * nq + i)
        s_in += specs
        s_args += [side.w, side.gain.reshape(*side.gain.shape, 1)]
        s_out.append(out_spec)
        s_shape.append(out_shape)
    out, *cast = pl.pallas_call(
        functools.partial(_attn_kernel, tq=tq, nheads=nheads, nsides=len(sides)),
        grid=(batch, hgroups, nq),
        in_specs=[pl.BlockSpec((tq, width), lambda b, h, i: (b * nq + i, h)),
                  pl.BlockSpec((seq, width), lambda b, h, i: (b, hgroups + h)),
                  pl.BlockSpec((seq, width), lambda b, h, i: (b, 2 * hgroups + h)),
                  pl.BlockSpec((2 * tq, tq), lambda b, h, i: (0, 0))] + s_in,
        out_specs=[pl.BlockSpec((tq, width), lambda b, h, i: (b * nq + i, h))] + s_out,
        out_shape=[jax.ShapeDtypeStruct((batch * seq, heads * HEAD_DIM), BF16)] + s_shape,
        scratch_shapes=[pltpu.VMEM((tq, width), F32), pltpu.VMEM((nheads, tq, 1), F32)],
        compiler_params=_params("arbitrary", "arbitrary", "arbitrary"),
        name="stickbreak_attn",
    )(qkv, qkv, qkv, tri2, *s_args)
    return out, cast


def _ssm_kernel(u_ref, b_ref, m_ref, c_ref, d_ref, o_ref, x_ref, xb_ref, *, seq):
    ns = SLAB_STATES
    nchunks = seq // SSM_CHUNK
    steps = SSM_CHUNK // SUBLANES

    def rows(k):
        return slice(k * SSM_CHUNK, (k + 1) * SSM_CHUNK)

    def group(i):
        return slice(i * SUBLANES, (i + 1) * SUBLANES)

    def project(k):
        x_ref[rows(k), :] = _dot(u_ref[rows(k), :], b_ref[0])

    def scan(k, v):
        v_re, v_im = v
        for s in range(steps):
            r = group(k * steps + s)
            l_re = m_ref[0, 0]
            l_im = m_ref[0, 1]
            v_re, v_im = (l_re * v_re - l_im * v_im + x_ref[r, 0:ns],
                          l_re * v_im + l_im * v_re + x_ref[r, ns:2 * ns])
            x_ref[r, 0:ns] = v_re
            x_ref[r, ns:2 * ns] = v_im
        return v_re, v_im

    def carry_in(e):
        e_re, e_im = e
        for n, shift in enumerate((1, 2, 4)):
            a_re = m_ref[0, 2 + 2 * n]
            a_im = m_ref[0, 3 + 2 * n]
            s_re = pltpu.roll(e_re, shift, 0)
            s_im = pltpu.roll(e_im, shift, 0)
            e_re, e_im = (e_re + (a_re * s_re - a_im * s_im),
                          e_im + (a_re * s_im + a_im * s_re))
        not_first = m_ref[0, 8]
        return not_first * pltpu.roll(e_re, 1, 0), not_first * pltpu.roll(e_im, 1, 0)

    def fix(k, w):
        w_re, w_im = w
        for s in range(0, steps, 2):
            parts = []
            for i in (k * steps + s, k * steps + s + 1):
                l_re = m_ref[0, 0]
                l_im = m_ref[0, 1]
                w_re, w_im = l_re * w_re - l_im * w_im, l_re * w_im + l_im * w_re
                parts.append(jnp.concatenate([x_ref[group(i), 0:ns] + w_re,
                                              x_ref[group(i), ns:2 * ns] + w_im], axis=1))
            i0 = k * steps + s
            xb_ref[i0 * SUBLANES:(i0 + 2) * SUBLANES, :] = jnp.concatenate(parts, axis=0).astype(BF16)
        return w_re, w_im

    def readout(k):
        y = _dot(xb_ref[rows(k), :], c_ref[0]) + d_ref[...] * u_ref[rows(k), :].astype(F32)
        o_ref[rows(k), :] = jax.nn.gelu(y, approximate=True).astype(o_ref.dtype)

    project(0)
    if nchunks > 1:
        project(1)
    zero = jnp.zeros((SUBLANES, ns), F32)
    v = scan(0, (zero, zero))
    for k in range(1, nchunks):
        if k + 1 < nchunks:
            project(k + 1)
        v = scan(k, v)
    w = fix(0, carry_in(v))
    for k in range(1, nchunks):
        w = fix(k, w)
        readout(k - 1)
    readout(nchunks - 1)


def _ssm_operands(lam_re, lam_im, log_dt, b_re, b_im, c_re, c_im, sub_len):
    g = lam_re.shape[0]
    nslab = g // SLAB_GROUPS
    dt = jnp.exp(log_dt)[:, None]
    lam = lax.complex(lam_re, lam_im)
    lam_bar = jnp.exp(lam * dt)
    b_bar = ((lam_bar - 1.0) / lam)[..., None] * lax.complex(b_re, b_im)
    eye = jnp.eye(SLAB_GROUPS, dtype=F32)

    def b_block(part):
        t = part.reshape(nslab, SLAB_GROUPS, SSM_STATE, SSM_GROUP).transpose(0, 1, 3, 2)
        return jnp.einsum("sgcp,gh->sgchp", t, eye).reshape(nslab, SLAB, SLAB_STATES)

    def c_block(part):
        t = part.reshape(nslab, SLAB_GROUPS, SSM_GROUP, SSM_STATE).transpose(0, 1, 3, 2)
        return jnp.einsum("sgpc,gh->sgphc", t, eye).reshape(nslab, SLAB_STATES, SLAB)

    b_mat = jnp.concatenate([b_block(b_bar.real), b_block(b_bar.imag)], axis=2).astype(BF16)
    c_mat = jnp.concatenate([c_block(c_re), c_block(-c_im)], axis=1).astype(BF16)

    lam1 = lam_bar.reshape(nslab, 1, SLAB_STATES)
    big1, sq, n = jnp.ones_like(lam1), lam1, sub_len
    while n:
        if n & 1:
            big1 = big1 * sq
        sq, n = sq * sq, n >> 1
    big2 = big1 * big1
    big4 = big2 * big2
    rows = jnp.arange(SUBLANES)[None, :, None]
    every = jnp.broadcast_to(lam1, (nslab, SUBLANES, SLAB_STATES))
    planes = [every.real, every.imag]
    for big, shift in ((big1, 1), (big2, 2), (big4, 4)):
        a = jnp.where(rows >= shift, big, 0.0)
        planes += [a.real, a.imag]
    planes.append(jnp.broadcast_to((rows >= 1).astype(F32), (nslab, SUBLANES, SLAB_STATES)))
    mult = jnp.stack(planes, axis=1).astype(F32)
    return b_mat, mult, c_mat


def _ssm(u_perm, b_mat, mult, c_mat, d_skip, layer, *, batch, seq):
    nslab = b_mat.shape[1]
    width = nslab * SLAB
    sub_len = seq // SUBLANES
    assert seq % SSM_CHUNK == 0 and (SSM_CHUNK // SUBLANES) % 2 == 0
    return pl.pallas_call(
        functools.partial(_ssm_kernel, seq=seq),
        grid=(batch, nslab),
        in_specs=[pl.BlockSpec((seq, SLAB), lambda b, s: (b, s)),
                  pl.BlockSpec((None, 1, SLAB, 2 * SLAB_STATES), lambda b, s: (layer, s, 0, 0)),
                  pl.BlockSpec((None, 1, 9, SUBLANES, SLAB_STATES), lambda b, s: (layer, s, 0, 0, 0)),
                  pl.BlockSpec((None, 1, 2 * SLAB_STATES, SLAB), lambda b, s: (layer, s, 0, 0)),
                  pl.BlockSpec((None, 1, SLAB), lambda b, s: (layer, 0, s))],
        out_specs=pl.BlockSpec((seq, SLAB), lambda b, s: (b, s)),
        out_shape=jax.ShapeDtypeStruct((batch * seq, width), BF16),
        scratch_shapes=[pltpu.VMEM((seq, 2 * SLAB_STATES), F32), pltpu.VMEM((seq, 2 * SLAB_STATES), BF16)],
        compiler_params=_params("parallel", "parallel"),
        name="s5_ssm",
    )(u_perm, b_mat, mult, c_mat, d_skip.reshape(-1, 1, width))


def kernel(x, p, g_mix, w_in, w_br_attn, lam_re, lam_im, log_dt, b_re, b_im, c_re, c_im, d_skip, w_glu, w_br_ssm, w_o, g_mlp, w_ff1, w_ff2, g_ple, w_ple_gate, w_ple, g_final):
    bsz, seq, d = x.shape
    depth = w_in.shape[0]
    attn_w = w_br_attn.shape[1]
    ssm_w = w_br_ssm.shape[1]
    heads = attn_w // HEAD_DIM
    t = bsz * seq
    assert w_in.shape[2] == 3 * attn_w + ssm_w + 2 * d

    wb_in = _to_bf16(w_in, g_mix, layer=0)
    wb_ple = _to_bf16(w_ple)
    b_mat, mult, c_mat = jax.vmap(functools.partial(_ssm_operands, sub_len=seq // SUBLANES))(
        lam_re, lam_im, log_dt, b_re, b_im, c_re, c_im)

    def plain(w):
        return jnp.ones(w.shape[:2], F32)

    h = x.reshape(t, d)
    hb, ss = _enter(h)
    for i in range(depth):
        qkv, gates = _proj(hb, ss, wb_in, i, attn_w=attn_w, ssm_w=ssm_w, d_model=d)
        u_perm = _proj_u(hb, ss, wb_in, i, col0=3 * attn_w, width=ssm_w, batch=bsz, seq=seq)
        attn, (wb_ff1, wb_o, wb_ple_gate, wb_br_attn, wb_br_ssm, wb_glu) = _attention(
            qkv,
            [_SideCast(w_ff1, g_mlp, i), _SideCast(w_o, plain(w_o), i), _SideCast(w_ple_gate, g_ple, i),
             _SideCast(w_br_attn, plain(w_br_attn), i), _SideCast(w_br_ssm, plain(w_br_ssm), i),
             _SideCast(w_glu, plain(w_glu), i)],
            batch=bsz, seq=seq, heads=heads)
        y_perm = _ssm(u_perm, b_mat, mult, c_mat, d_skip, i, batch=bsz, seq=seq)
        ssm3 = _glu(y_perm, wb_glu, i, batch=bsz, seq=seq)
        merged = _merge(attn, wb_br_attn, ssm3, wb_br_ssm, gates, i, seq=seq)
        h, hb, ss = _resid_matmul(merged, wb_o, i, h, name="out_proj")

        act, wb_ff2 = _normed_relu2_matmul(hb, ss, wb_ff1, i, name="ff1",
                                           side=_SideCast(w_ff2, plain(w_ff2), i))
        if i + 1 < depth:
            h, hb, ss, wb_in = _resid_matmul(act, wb_ff2, i, h, name="ff2",
                                             side=_SideCast(w_in, g_mix, i + 1))
        else:
            h, hb, ss = _resid_matmul(act, wb_ff2, i, h, name="ff2")

        h, hb, ss = _ple(hb, ss, wb_ple_gate, p[i].reshape(t, -1), wb_ple, i, h)
    return _rmsnorm(h, g_final, x.dtype).reshape(bsz, seq, d)
```

```python
import functools
import math
from typing import NamedTuple

import jax
import jax.numpy as jnp
from jax import lax
from jax.experimental import pallas as pl
from jax.experimental.pallas import tpu as pltpu

F32 = jnp.float32
BF16 = jnp.bfloat16

RMS_EPS = 1e-6
HEAD_DIM = 128
ATTN_Q_SCALE = HEAD_DIM ** -0.5 * math.log2(math.e)
ATTN_HEADS_PER_STEP = 8
SSM_GROUP = 16
SSM_STATE = 64
SLAB = 256
SLAB_GROUPS = SLAB // SSM_GROUP
SLAB_STATES = SLAB_GROUPS * SSM_STATE
SUBLANES = 8
LANES = 128
SSM_CHUNK = 512
V7X_VMEM_LIMIT = 56 * 1024 * 1024
CAST_BLOCK_BYTES = 8 * 1024 * 1024


def _tile(dim, pref):
    if dim % pref == 0:
        return pref
    assert dim < pref, (dim, pref)
    return dim


def _params(*sem):
    return pltpu.CompilerParams(dimension_semantics=sem, vmem_limit_bytes=V7X_VMEM_LIMIT)


def _dot(a, b):
    return jnp.dot(a, b, preferred_element_type=F32)


def _wspec(w, layer, k, tn, joff=0):
    if w.ndim == 2:
        return pl.BlockSpec((k, tn), lambda i, j, *kk: (kk[0] if kk else 0, j + joff))
    return pl.BlockSpec((None, k, tn), lambda i, j, *kk: (layer, kk[0] if kk else 0, j + joff))


class _SideCast(NamedTuple):
    w: jax.Array
    gain: jax.Array
    layer: int


def _side_cast_specs(side, nsteps, step_of):
    depth, k, n = side.w.shape
    nblk = max(b for b in range(1, nsteps + 1) if k % b == 0 and (k // b) % 16 == 0)
    bk = k // nblk

    def blk(*g):
        return jnp.minimum(step_of(*g), nblk - 1)

    in_specs = [pl.BlockSpec((None, bk, n), lambda *g: (side.layer, blk(*g), 0)),
                pl.BlockSpec((None, bk, 1), lambda *g: (side.layer, blk(*g), 0))]
    return in_specs, pl.BlockSpec((bk, n), lambda *g: (blk(*g), 0)), jax.ShapeDtypeStruct((k, n), BF16)


def _cast_block(w_ref, g_ref, o_ref):
    o_ref[...] = (w_ref[...] * g_ref[...]).astype(o_ref.dtype)


def _rstd(ss_ref, d):
    return lax.rsqrt(ss_ref[...] * (1.0 / d) + RMS_EPS)


def _emit_stream(h, h_ref, hb_ref, ss_ref, j):
    h_ref[...] = h
    hb_ref[...] = h.astype(BF16)
    row_ss = jnp.sum(h * h, axis=1, keepdims=True)

    @pl.when(j == 0)
    def _():
        ss_ref[...] = row_ss

    @pl.when(j > 0)
    def _():
        ss_ref[...] += row_ss


def _stream_out(t, n, tm, tn):
    specs = [pl.BlockSpec((tm, tn), lambda i, j, *kk: (i, j)),
             pl.BlockSpec((tm, tn), lambda i, j, *kk: (i, j)),
             pl.BlockSpec((tm, 1), lambda i, j, *kk: (i, 0))]
    shapes = [jax.ShapeDtypeStruct((t, n), F32), jax.ShapeDtypeStruct((t, n), BF16),
              jax.ShapeDtypeStruct((t, 1), F32)]
    return specs, shapes


def _to_bf16(w, gain=None, layer=None):
    depth, k, n = w.shape
    bk = max(16, min(k, CAST_BLOCK_BYTES // (4 * n)))
    assert k % bk == 0, (k, bk)
    if gain is None:
        gain = jnp.ones((depth, k), F32)
    if layer is None:
        grid, first = (depth, k // bk), 0
        out_spec = pl.BlockSpec((None, bk, n), lambda a, r: (a, r, 0))
        out_shape = jax.ShapeDtypeStruct(w.shape, BF16)
    else:
        grid, first = (1, k // bk), layer
        out_spec = pl.BlockSpec((bk, n), lambda a, r: (r, 0))
        out_shape = jax.ShapeDtypeStruct((k, n), BF16)
    return pl.pallas_call(
        _cast_block,
        grid=grid,
        in_specs=[pl.BlockSpec((None, bk, n), lambda a, r: (a + first, r, 0)),
                  pl.BlockSpec((None, bk, 1), lambda a, r: (a + first, r, 0))],
        out_specs=out_spec,
        out_shape=out_shape,
        compiler_params=_params("parallel", "parallel"),
        name="cast_bf16",
    )(w, gain.reshape(depth, k, 1))


def _enter_kernel(x_ref, xb_ref, ss_ref):
    x = x_ref[...]
    xb_ref[...] = x.astype(BF16)
    ss_ref[...] = jnp.sum(x * x, axis=1, keepdims=True)


def _enter(x):
    t, d = x.shape
    tm = _tile(t, 256)
    return pl.pallas_call(
        _enter_kernel,
        grid=(t // tm,),
        in_specs=[pl.BlockSpec((tm, d), lambda i: (i, 0))],
        out_specs=[pl.BlockSpec((tm, d), lambda i: (i, 0)), pl.BlockSpec((tm, 1), lambda i: (i, 0))],
        out_shape=[jax.ShapeDtypeStruct((t, d), BF16), jax.ShapeDtypeStruct((t, 1), F32)],
        compiler_params=_params("parallel"),
        name="enter",
    )(x)


def _rmsnorm_kernel(x_ref, g_ref, o_ref):
    x = x_ref[...]
    ms = jnp.mean(x * x, axis=-1, keepdims=True)
    o_ref[...] = (x * lax.rsqrt(ms + RMS_EPS) * g_ref[...]).astype(o_ref.dtype)


def _rmsnorm(x, g, out_dtype):
    t, d = x.shape
    tm = _tile(t, 256)
    return pl.pallas_call(
        _rmsnorm_kernel,
        grid=(t // tm,),
        in_specs=[pl.BlockSpec((tm, d), lambda i: (i, 0)),
                  pl.BlockSpec((1, d), lambda i: (0, 0))],
        out_specs=pl.BlockSpec((tm, d), lambda i: (i, 0)),
        out_shape=jax.ShapeDtypeStruct((t, d), out_dtype),
        compiler_params=_params("parallel"),
        name="rmsnorm",
    )(x, g.reshape(1, d))


def _normed_mm_kernel(hb_ref, w_ref, ss_ref, *rest, d, side):
    if side:
        sw_ref, sg_ref, o_ref, so_ref = rest
        _cast_block(sw_ref, sg_ref, so_ref)
    else:
        (o_ref,) = rest
    acc = _dot(hb_ref[...], w_ref[...]) * _rstd(ss_ref, d)
    o_ref[...] = jnp.square(jnp.maximum(acc, 0.0)).astype(o_ref.dtype)


def _normed_relu2_matmul(hb, ss, w, layer, *, name, side=None):
    t, k = hb.shape
    ncols = w.shape[-1]
    tm, tn = _tile(t, 1024), _tile(ncols, 1024)
    nj = ncols // tn
    in_specs = [pl.BlockSpec((tm, k), lambda i, j: (i, 0)),
                _wspec(w, layer, k, tn),
                pl.BlockSpec((tm, 1), lambda i, j: (i, 0))]
    args = [hb, w, ss]
    out_specs = [pl.BlockSpec((tm, tn), lambda i, j: (i, j))]
    out_shape = [jax.ShapeDtypeStruct((t, ncols), BF16)]
    if side is not None:
        s_in, s_out, s_shape = _side_cast_specs(side, (t // tm) * nj, lambda i, j: i * nj + j)
        in_specs += s_in
        args += [side.w, side.gain.reshape(*side.gain.shape, 1)]
        out_specs.append(s_out)
        out_shape.append(s_shape)
    outs = pl.pallas_call(
        functools.partial(_normed_mm_kernel, d=k, side=side is not None),
        grid=(t // tm, nj),
        in_specs=in_specs,
        out_specs=out_specs,
        out_shape=out_shape,
        compiler_params=_params("parallel", "parallel"),
        name=name,
    )(*args)
    return outs if side is not None else outs[0]


def _proj_kernel(hb_ref, w_ref, ss_ref, qkv_ref, gates_ref, *, d, nq, nqkv):
    j = pl.program_id(1)

    @pl.when(j < nqkv)
    def _():
        scale = jnp.where(j < nq, ATTN_Q_SCALE, 1.0)
        qkv_ref[...] = (_dot(hb_ref[...], w_ref[...]) * (_rstd(ss_ref, d) * scale)).astype(qkv_ref.dtype)

    @pl.when(j >= nqkv)
    def _():
        gates_ref[...] = jax.nn.sigmoid(_dot(hb_ref[...], w_ref[...]) * _rstd(ss_ref, d)).astype(gates_ref.dtype)


def _proj(hb, ss, w, layer, *, attn_w, ssm_w, d_model):
    t, k = hb.shape
    tm = _tile(t, 1024)
    tn = next(c for c in (1024, 512, 256, 128)
              if attn_w % c == 0 and ssm_w % c == 0 and (2 * d_model) % c == 0)
    nq, nqkv, nu, ng = attn_w // tn, 3 * attn_w // tn, ssm_w // tn, 2 * d_model // tn

    def wcol(j):
        return jnp.where(j < nqkv, j, j + nu)

    if w.ndim == 2:
        w_spec = pl.BlockSpec((k, tn), lambda i, j: (0, wcol(j)))
    else:
        w_spec = pl.BlockSpec((None, k, tn), lambda i, j: (layer, 0, wcol(j)))
    return pl.pallas_call(
        functools.partial(_proj_kernel, d=k, nq=nq, nqkv=nqkv),
        grid=(t // tm, nqkv + ng),
        in_specs=[pl.BlockSpec((tm, k), lambda i, j: (i, 0)),
                  w_spec,
                  pl.BlockSpec((tm, 1), lambda i, j: (i, 0))],
        out_specs=[pl.BlockSpec((tm, tn), lambda i, j: (i, jnp.minimum(j, nqkv - 1))),
                   pl.BlockSpec((tm, tn), lambda i, j: (i, jnp.maximum(j - nqkv, 0)))],
        out_shape=[jax.ShapeDtypeStruct((t, 3 * attn_w), BF16),
                   jax.ShapeDtypeStruct((t, 2 * d_model), BF16)],
        compiler_params=_params("parallel", "arbitrary"),
        name="proj",
    )(hb, w, ss)


def _proj_u_kernel(hb_ref, w_ref, ss_ref, o_ref, stage_ref, *, d, sub_len):
    acc = _dot(hb_ref[...], w_ref[...]) * _rstd(ss_ref, d)
    for c in range(acc.shape[1] // LANES):
        for r in range(SUBLANES):
            stage_ref[c, pl.ds(r, sub_len, stride=SUBLANES), :] = (
                acc[r * sub_len:(r + 1) * sub_len, c * LANES:(c + 1) * LANES])
    for c in range(acc.shape[1] // LANES):
        o_ref[:, c * LANES:(c + 1) * LANES] = stage_ref[c].astype(o_ref.dtype)


def _proj_u(hb, ss, w, layer, *, col0, width, batch, seq):
    t, k = hb.shape
    sub_len = seq // SUBLANES
    tn = _tile(width, 256)
    assert col0 % tn == 0
    joff = col0 // tn
    if w.ndim == 2:
        w_spec = pl.BlockSpec((k, tn), lambda b, j: (0, j + joff))
    else:
        w_spec = pl.BlockSpec((None, k, tn), lambda b, j: (layer, 0, j + joff))
    return pl.pallas_call(
        functools.partial(_proj_u_kernel, d=k, sub_len=sub_len),
        grid=(batch, width // tn),
        in_specs=[pl.BlockSpec((seq, k), lambda b, j: (b, 0)),
                  w_spec,
                  pl.BlockSpec((seq, 1), lambda b, j: (b, 0))],
        out_specs=pl.BlockSpec((seq, tn), lambda b, j: (b, j)),
        out_shape=jax.ShapeDtypeStruct((t, width), BF16),
        scratch_shapes=[pltpu.VMEM((tn // LANES, seq, LANES), F32)],
        compiler_params=_params("parallel", "parallel"),
        name="proj_u",
    )(hb, w, ss)


def _glu_kernel(y_ref, w_ref, o_ref, stage_ref, *, width):
    y = y_ref[...]
    res = y.astype(F32) * jax.nn.sigmoid(_dot(y, w_ref[...]))
    ntile = width // LANES
    sub_rows = res.shape[0] // SUBLANES
    for c in range(ntile):
        stage_ref[c] = res[:, c * LANES:(c + 1) * LANES]
    for r in range(SUBLANES):
        for c in range(ntile):
            lane0 = r * width + c * LANES
            o_ref[:, lane0:lane0 + LANES] = (
                stage_ref[c, pl.ds(r, sub_rows, stride=SUBLANES), :].astype(o_ref.dtype))


def _glu(y_perm, w, layer, *, batch, seq):
    t, width = y_perm.shape
    assert w.shape[-1] == width and w.shape[-2] == width
    tm = _tile(seq, 512)
    tiles = seq // tm
    return pl.pallas_call(
        functools.partial(_glu_kernel, width=width),
        grid=(t // tm,),
        in_specs=[pl.BlockSpec((tm, width), lambda i: (i, 0)),
                  (pl.BlockSpec((width, width), lambda i: (0, 0)) if w.ndim == 2
                   else pl.BlockSpec((None, width, width), lambda i: (layer, 0, 0)))],
        out_specs=pl.BlockSpec((None, tm // SUBLANES, SUBLANES * width), lambda i: (i // tiles, i % tiles, 0)),
        out_shape=jax.ShapeDtypeStruct((batch, seq // SUBLANES, SUBLANES * width), BF16),
        scratch_shapes=[pltpu.VMEM((width // LANES, tm, LANES), F32)],
        compiler_params=_params("parallel"),
        name="glu",
    )(y_perm, w)


def _merge_kernel(a_ref, wa_ref, s_ref, ws_ref, ga_ref, gs_ref, o_ref, *, nr, sub_len, width):
    up_a = _dot(a_ref[...], wa_ref[...])
    for rl in range(nr):
        rows = slice(rl * sub_len, (rl + 1) * sub_len)
        up_s = _dot(s_ref[:, rl * width:(rl + 1) * width], ws_ref[...])
        o_ref[rows, :] = (ga_ref[rows, :].astype(F32) * up_a[rows, :]
                          + gs_ref[rows, :].astype(F32) * up_s).astype(o_ref.dtype)


def _merge(attn, wa, ssm3, ws, gates, layer, *, seq):
    t, ka = attn.shape
    sub_len = seq // SUBLANES
    assert ssm3.shape[1] == sub_len
    ks = ssm3.shape[2] // SUBLANES
    d = wa.shape[-1]
    tm, tn = _tile(seq, 1024), _tile(d, 1024)
    assert tm % sub_len == 0
    nr = tm // sub_len
    tiles = seq // tm
    nj = d // tn
    return pl.pallas_call(
        functools.partial(_merge_kernel, nr=nr, sub_len=sub_len, width=ks),
        grid=(t // tm, nj),
        in_specs=[pl.BlockSpec((tm, ka), lambda i, j: (i, 0)),
                  _wspec(wa, layer, ka, tn),
                  pl.BlockSpec((None, sub_len, nr * ks), lambda i, j: (i // tiles, 0, i % tiles)),
                  _wspec(ws, layer, ks, tn),
                  pl.BlockSpec((tm, tn), lambda i, j: (i, j)),
                  pl.BlockSpec((tm, tn), lambda i, j: (i, j + nj))],
        out_specs=pl.BlockSpec((tm, tn), lambda i, j: (i, j)),
        out_shape=jax.ShapeDtypeStruct((t, d), BF16),
        compiler_params=_params("parallel", "parallel"),
        name="merge",
    )(attn, wa, ssm3, ws, gates, gates)


def _resid_mm_kernel(x_ref, w_ref, h_ref, *rest, nk, side):
    if side:
        sw_ref, sg_ref, o_ref, ob_ref, ss_ref, so_ref = rest
        _cast_block(sw_ref, sg_ref, so_ref)
    else:
        o_ref, ob_ref, ss_ref = rest
    j = pl.program_id(1)
    if nk == 1:
        _emit_stream(h_ref[...] + _dot(x_ref[...], w_ref[...]), o_ref, ob_ref, ss_ref, j)
        return
    k = pl.program_id(2)

    @pl.when(k == 0)
    def _():
        o_ref[...] = h_ref[...] + _dot(x_ref[...], w_ref[...])

    @pl.when(jnp.logical_and(k > 0, k < nk - 1))
    def _():
        o_ref[...] += _dot(x_ref[...], w_ref[...])

    @pl.when(k == nk - 1)
    def _():
        _emit_stream(o_ref[...] + _dot(x_ref[...], w_ref[...]), o_ref, ob_ref, ss_ref, j)


def _resid_matmul(x, w, layer, h, *, name, side=None):
    t, k = x.shape
    n = w.shape[-1]
    tk = k if k <= 4096 else _tile(k, 2048)
    nk = k // tk
    assert nk == 1 or nk >= 2
    tm, tn = _tile(t, 1024), _tile(n, 512 if nk == 1 else 1024)
    nj = n // tn
    in_specs = [pl.BlockSpec((tm, tk), lambda i, j, kk: (i, kk)),
                _wspec(w, layer, tk, tn),
                pl.BlockSpec((tm, tn), lambda i, j, kk: (i, j))]
    args = [x, w, h]
    out_specs, out_shape = _stream_out(t, n, tm, tn)
    if side is not None:
        s_in, s_out, s_shape = _side_cast_specs(side, (t // tm) * nj * nk,
                                                lambda i, j, kk: (i * nj + j) * nk + kk)
        in_specs += s_in
        args += [side.w, side.gain.reshape(*side.gain.shape, 1)]
        out_specs.append(s_out)
        out_shape.append(s_shape)
    return pl.pallas_call(
        functools.partial(_resid_mm_kernel, nk=nk, side=side is not None),
        grid=(t // tm, nj, nk),
        in_specs=in_specs,
        out_specs=out_specs,
        out_shape=out_shape,
        compiler_params=_params("parallel", "arbitrary", "arbitrary"),
        name=name,
    )(*args)


def _ple_kernel(hb_ref, wg_ref, ss_ref, p_ref, wp_ref, h_ref, o_ref, ob_ref, sso_ref, *, d):
    gate = jax.nn.sigmoid(_dot(hb_ref[...], wg_ref[...]) * _rstd(ss_ref, d))
    emb = _dot(p_ref[...].astype(BF16), wp_ref[...])
    _emit_stream(h_ref[...] + emb * gate, o_ref, ob_ref, sso_ref, pl.program_id(1))


def _ple(hb, ss, wg, p, wp, layer, h):
    t, k = hb.shape
    kp = p.shape[1]
    n = wg.shape[-1]
    tm, tn = _tile(t, 1024), _tile(n, 512)
    out_specs, out_shape = _stream_out(t, n, tm, tn)
    return pl.pallas_call(
        functools.partial(_ple_kernel, d=k),
        grid=(t // tm, n // tn),
        in_specs=[pl.BlockSpec((tm, k), lambda i, j: (i, 0)),
                  _wspec(wg, layer, k, tn),
                  pl.BlockSpec((tm, 1), lambda i, j: (i, 0)),
                  pl.BlockSpec((tm, kp), lambda i, j: (i, 0)),
                  _wspec(wp, layer, kp, tn),
                  pl.BlockSpec((tm, tn), lambda i, j: (i, j))],
        out_specs=out_specs,
        out_shape=out_shape,
        compiler_params=_params("parallel", "arbitrary"),
        name="ple",
    )(hb, wg, ss, p, wp, h)


def _attn_kernel(q_ref, k_ref, v_ref, tri_ref, *rest, tq, nheads, nsides):
    side_in, (o_ref, *side_out), (acc_ref, carry_ref) = rest[:2 * nsides], rest[2 * nsides:3 * nsides + 1], rest[3 * nsides + 1:]
    for n in range(nsides):
        _cast_block(side_in[2 * n], side_in[2 * n + 1], side_out[n])
    qi = pl.program_id(2)
    tri = tri_ref[...]
    row = lax.broadcasted_iota(jnp.int32, (tq, tq), 0)
    col = lax.broadcasted_iota(jnp.int32, (tq, tq), 1)
    causal = col < row

    def key_block(kb, masked):
        start = pl.multiple_of(kb * tq, tq)
        heads = range(nheads)
        lanes = [slice(hd * HEAD_DIM, (hd + 1) * HEAD_DIM) for hd in heads]
        zs = [lax.dot_general(q_ref[:, lanes[hd]], k_ref[pl.ds(start, tq), lanes[hd]],
                              (((1,), (1,)), ((), ())), preferred_element_type=F32) for hd in heads]
        tails = []
        for z in zs:
            neg_abs = lax.bitcast_convert_type(
                lax.bitcast_convert_type(z, jnp.uint32) | jnp.uint32(0x80000000), F32)
            softplus = jnp.maximum(z, 0.0) + jnp.log2(1.0 + jnp.exp2(neg_abs))
            if masked:
                softplus = jnp.where(causal, softplus, 0.0)
            hi = softplus.astype(BF16)
            lo = (softplus - hi.astype(F32)).astype(BF16)
            tails.append(_dot(jnp.concatenate([hi, lo], axis=1), tri))
        for hd in heads:
            w = jnp.exp2(zs[hd] + tails[hd] + carry_ref[hd])
            if masked:
                w = jnp.where(causal, w, 0.0)
            acc_ref[:, lanes[hd]] += _dot(w.astype(BF16), v_ref[pl.ds(start, tq), lanes[hd]])
            carry_ref[hd] += tails[hd][:, 0:1]

    acc_ref[...] = jnp.zeros_like(acc_ref)
    carry_ref[...] = jnp.zeros_like(carry_ref)
    key_block(qi, True)

    def body(i, c):
        key_block(qi - 1 - i, False)
        return c

    lax.fori_loop(0, qi, body, 0)
    o_ref[...] = acc_ref[...].astype(o_ref.dtype)


def _attention(qkv, sides, *, batch, seq, heads):
    tq = _tile(seq, 256)
    nq = seq // tq
    nheads = _tile(heads, ATTN_HEADS_PER_STEP)
    hgroups = heads // nheads
    width = nheads * HEAD_DIM
    r = jnp.arange(tq)
    tri = -(r[:, None] >= r[None, :]).astype(BF16)
    tri2 = jnp.concatenate([tri, tri], axis=0)
    nsteps = batch * hgroups * nq
    s_in, s_args, s_out, s_shape = [], [], [], []
    for side in sides:
        specs, out_spec, out_shape = _side_cast_specs(side, nsteps, lambda b, h, i: (b * hgroups + h) * nq + i)
        s_in += specs
        s_args += [side.w, side.gain.reshape(*side.gain.shape, 1)]
        s_out.append(out_spec)
        s_shape.append(out_shape)
    out, *cast = pl.pallas_call(
        functools.partial(_attn_kernel, tq=tq, nheads=nheads, nsides=len(sides)),
        grid=(batch, hgroups, nq),
        in_specs=[pl.BlockSpec((tq, width), lambda b, h, i: (b * nq + i, h)),
                  pl.BlockSpec((seq, width), lambda b, h, i: (b, hgroups + h)),
                  pl.BlockSpec((seq, width), lambda b, h, i: (b, 2 * hgroups + h)),
                  pl.BlockSpec((2 * tq, tq), lambda b, h, i: (0, 0))] + s_in,
        out_specs=[pl.BlockSpec((tq, width), lambda b, h, i: (b * nq + i, h))] + s_out,
        out_shape=[jax.ShapeDtypeStruct((batch * seq, heads * HEAD_DIM), BF16)] + s_shape,
        scratch_shapes=[pltpu.VMEM((tq, width), F32), pltpu.VMEM((nheads, tq, 1), F32)],
        compiler_params=_params("arbitrary", "arbitrary", "arbitrary"),
        name="stickbreak_attn",
    )(qkv, qkv, qkv, tri2, *s_args)
    return out, cast


def _ssm_kernel(u_ref, b_ref, m_ref, c_ref, d_ref, o_ref, x_ref, xb_ref, *, seq):
    ns = SLAB_STATES
    nchunks = seq // SSM_CHUNK
    steps = SSM_CHUNK // SUBLANES

    def rows(k):
        return slice(k * SSM_CHUNK, (k + 1) * SSM_CHUNK)

    def group(i):
        return slice(i * SUBLANES, (i + 1) * SUBLANES)

    def project(k):
        x_ref[rows(k), :] = _dot(u_ref[rows(k), :], b_ref[0])

    def scan(k, v):
        v_re, v_im = v
        for s in range(steps):
            r = group(k * steps + s)
            l_re = m_ref[0, 0]
            l_im = m_ref[0, 1]
            v_re, v_im = (l_re * v_re - l_im * v_im + x_ref[r, 0:ns],
                          l_re * v_im + l_im * v_re + x_ref[r, ns:2 * ns])
            x_ref[r, 0:ns] = v_re
            x_ref[r, ns:2 * ns] = v_im
        return v_re, v_im

    def carry_in(e):
        e_re, e_im = e
        for n, shift in enumerate((1, 2, 4)):
            a_re = m_ref[0, 2 + 2 * n]
            a_im = m_ref[0, 3 + 2 * n]
            s_re = pltpu.roll(e_re, shift, 0)
            s_im = pltpu.roll(e_im, shift, 0)
            e_re, e_im = (e_re + (a_re * s_re - a_im * s_im),
                          e_im + (a_re * s_im + a_im * s_re))
        not_first = m_ref[0, 8]
        return not_first * pltpu.roll(e_re, 1, 0), not_first * pltpu.roll(e_im, 1, 0)

    def fix(k, w):
        w_re, w_im = w
        for s in range(0, steps, 2):
            parts = []
            for i in (k * steps + s, k * steps + s + 1):
                l_re = m_ref[0, 0]
                l_im = m_ref[0, 1]
                w_re, w_im = l_re * w_re - l_im * w_im, l_re * w_im + l_im * w_re
                parts.append(jnp.concatenate([x_ref[group(i), 0:ns] + w_re,
                                              x_ref[group(i), ns:2 * ns] + w_im], axis=1))
            i0 = k * steps + s
            xb_ref[i0 * SUBLANES:(i0 + 2) * SUBLANES, :] = jnp.concatenate(parts, axis=0).astype(BF16)
        return w_re, w_im

    def readout(k):
        y = _dot(xb_ref[rows(k), :], c_ref[0]) + d_ref[...] * u_ref[rows(k), :].astype(F32)
        o_ref[rows(k), :] = jax.nn.gelu(y, approximate=True).astype(o_ref.dtype)

    project(0)
    if nchunks > 1:
        project(1)
    zero = jnp.zeros((SUBLANES, ns), F32)
    v = scan(0, (zero, zero))
    for k in range(1, nchunks):
        if k + 1 < nchunks:
            project(k + 1)
        v = scan(k, v)
    w = fix(0, carry_in(v))
    for k in range(1, nchunks):
        w = fix(k, w)
        readout(k - 1)
    readout(nchunks - 1)


def _ssm_operands(lam_re, lam_im, log_dt, b_re, b_im, c_re, c_im, sub_len):
    g = lam_re.shape[0]
    nslab = g // SLAB_GROUPS
    dt = jnp.exp(log_dt)[:, None]
    lam = lax.complex(lam_re, lam_im)
    lam_bar = jnp.exp(lam * dt)
    b_bar = ((lam_bar - 1.0) / lam)[..., None] * lax.complex(b_re, b_im)

    def block_diag(rows, per_row_group, width, per_col_group):
        n = rows.shape[-1]
        q = jnp.arange(width)
        tiled = jnp.dot(rows, (q[None, :] % n == jnp.arange(n)[:, None]).astype(F32))
        same = (jnp.arange(rows.shape[1]) // per_row_group)[:, None] == (q // per_col_group)[None, :]
        return jnp.where(same, tiled, 0.0)

    def b_block(part):
        t = part.reshape(nslab, SLAB_GROUPS, SSM_STATE, SSM_GROUP).transpose(0, 1, 3, 2)
        return block_diag(t.reshape(nslab, SLAB, SSM_STATE), SSM_GROUP, SLAB_STATES, SSM_STATE)

    def c_block(part):
        t = part.reshape(nslab, SLAB_GROUPS, SSM_GROUP, SSM_STATE).transpose(0, 1, 3, 2)
        return block_diag(t.reshape(nslab, SLAB_STATES, SSM_GROUP), SSM_STATE, SLAB, SSM_GROUP)

    b_mat = jnp.concatenate([b_block(b_bar.real), b_block(b_bar.imag)], axis=2).astype(BF16)
    c_mat = jnp.concatenate([c_block(c_re), c_block(-c_im)], axis=1).astype(BF16)

    lam1 = lam_bar.reshape(nslab, 1, SLAB_STATES)
    big1, sq, n = jnp.ones_like(lam1), lam1, sub_len
    while n:
        if n & 1:
            big1 = big1 * sq
        sq, n = sq * sq, n >> 1
    big2 = big1 * big1
    big4 = big2 * big2
    rows = jnp.arange(SUBLANES)[None, :, None]
    every = jnp.broadcast_to(lam1, (nslab, SUBLANES, SLAB_STATES))
    planes = [every.real, every.imag]
    for big, shift in ((big1, 1), (big2, 2), (big4, 4)):
        a = jnp.where(rows >= shift, big, 0.0)
        planes += [a.real, a.imag]
    planes.append(jnp.broadcast_to((rows >= 1).astype(F32), (nslab, SUBLANES, SLAB_STATES)))
    mult = jnp.stack(planes, axis=1).astype(F32)
    return b_mat, mult, c_mat


def _ssm(u_perm, b_mat, mult, c_mat, d_skip, layer, *, batch, seq):
    nslab = b_mat.shape[1]
    width = nslab * SLAB
    sub_len = seq // SUBLANES
    assert seq % SSM_CHUNK == 0 and (SSM_CHUNK // SUBLANES) % 2 == 0
    return pl.pallas_call(
        functools.partial(_ssm_kernel, seq=seq),
        grid=(batch, nslab),
        in_specs=[pl.BlockSpec((seq, SLAB), lambda b, s: (b, s)),
                  pl.BlockSpec((None, 1, SLAB, 2 * SLAB_STATES), lambda b, s: (layer, s, 0, 0)),
                  pl.BlockSpec((None, 1, 9, SUBLANES, SLAB_STATES), lambda b, s: (layer, s, 0, 0, 0)),
                  pl.BlockSpec((None, 1, 2 * SLAB_STATES, SLAB), lambda b, s: (layer, s, 0, 0)),
                  pl.BlockSpec((None, 1, SLAB), lambda b, s: (layer, 0, s))],
        out_specs=pl.BlockSpec((seq, SLAB), lambda b, s: (b, s)),
        out_shape=jax.ShapeDtypeStruct((batch * seq, width), BF16),
        scratch_shapes=[pltpu.VMEM((seq, 2 * SLAB_STATES), F32), pltpu.VMEM((seq, 2 * SLAB_STATES), BF16)],
        compiler_params=_params("parallel", "parallel"),
        name="s5_ssm",
    )(u_perm, b_mat, mult, c_mat, d_skip.reshape(-1, 1, width))


def kernel(x, p, g_mix, w_in, w_br_attn, lam_re, lam_im, log_dt, b_re, b_im, c_re, c_im, d_skip, w_glu, w_br_ssm, w_o, g_mlp, w_ff1, w_ff2, g_ple, w_ple_gate, w_ple, g_final):
    bsz, seq, d = x.shape
    depth = w_in.shape[0]
    attn_w = w_br_attn.shape[1]
    ssm_w = w_br_ssm.shape[1]
    heads = attn_w // HEAD_DIM
    t = bsz * seq
    assert w_in.shape[2] == 3 * attn_w + ssm_w + 2 * d

    wb_in = _to_bf16(w_in, g_mix, layer=0)
    wb_ple = _to_bf16(w_ple)
    b_mat, mult, c_mat = jax.vmap(functools.partial(_ssm_operands, sub_len=seq // SUBLANES))(
        lam_re, lam_im, log_dt, b_re, b_im, c_re, c_im)

    def plain(w):
        return jnp.ones(w.shape[:2], F32)

    h = x.reshape(t, d)
    hb, ss = _enter(h)
    for i in range(depth):
        qkv, gates = _proj(hb, ss, wb_in, i, attn_w=attn_w, ssm_w=ssm_w, d_model=d)
        u_perm = _proj_u(hb, ss, wb_in, i, col0=3 * attn_w, width=ssm_w, batch=bsz, seq=seq)
        attn, (wb_ff1, wb_o, wb_ple_gate, wb_br_attn, wb_br_ssm, wb_glu) = _attention(
            qkv,
            [_SideCast(w_ff1, g_mlp, i), _SideCast(w_o, plain(w_o), i), _SideCast(w_ple_gate, g_ple, i),
             _SideCast(w_br_attn, plain(w_br_attn), i), _SideCast(w_br_ssm, plain(w_br_ssm), i),
             _SideCast(w_glu, plain(w_glu), i)],
            batch=bsz, seq=seq, heads=heads)
        y_perm = _ssm(u_perm, b_mat, mult, c_mat, d_skip, i, batch=bsz, seq=seq)
        ssm3 = _glu(y_perm, wb_glu, i, batch=bsz, seq=seq)
        merged = _merge(attn, wb_br_attn, ssm3, wb_br_ssm, gates, i, seq=seq)
        h, hb, ss = _resid_matmul(merged, wb_o, i, h, name="out_proj")

        act, wb_ff2 = _normed_relu2_matmul(hb, ss, wb_ff1, i, name="ff1",
                                           side=_SideCast(w_ff2, plain(w_ff2), i))
        if i + 1 < depth:
            h, hb, ss, wb_in = _resid_matmul(act, wb_ff2, i, h, name="ff2",
                                             side=_SideCast(w_in, g_mix, i + 1))
        else:
            h, hb, ss = _resid_matmul(act, wb_ff2, i, h, name="ff2")

        h, hb, ss = _ple(hb, ss, wb_ple_gate, p[i].reshape(t, -1), wb_ple, i, h)
    return _rmsnorm(h, g_final, x.dtype).reshape(bsz, seq, d)
```

```python
import functools
import math
from typing import NamedTuple

import jax
import jax.numpy as jnp
from jax import lax
from jax.experimental import pallas as pl
from jax.experimental.pallas import tpu as pltpu

F32 = jnp.float32
BF16 = jnp.bfloat16

RMS_EPS = 1e-6
HEAD_DIM = 128
ATTN_Q_SCALE = HEAD_DIM ** -0.5 * math.log2(math.e)
ATTN_HEADS_PER_STEP = 8
SSM_GROUP = 16
SSM_STATE = 64
SLAB = 256
SLAB_GROUPS = SLAB // SSM_GROUP
SLAB_STATES = SLAB_GROUPS * SSM_STATE
SUBLANES = 8
LANES = 128
MXU_COLUMNS = 256
EPILOGUE_PARTS = 2
SSM_CHUNK = 512
V7X_VMEM_LIMIT = 56 * 1024 * 1024
CAST_BLOCK_BYTES = 8 * 1024 * 1024


def _tile(dim, pref):
    if dim % pref == 0:
        return pref
    assert dim < pref, (dim, pref)
    return dim


def _params(*sem):
    return pltpu.CompilerParams(dimension_semantics=sem, vmem_limit_bytes=V7X_VMEM_LIMIT)


def _dot(a, b):
    return jnp.dot(a, b, preferred_element_type=F32)


def _wspec(w, layer, k, tn, joff=0):
    if w.ndim == 2:
        return pl.BlockSpec((k, tn), lambda i, j, *kk: (kk[0] if kk else 0, j + joff))
    return pl.BlockSpec((None, k, tn), lambda i, j, *kk: (layer, kk[0] if kk else 0, j + joff))


class _SideCast(NamedTuple):
    w: jax.Array
    gain: jax.Array
    layer: int
    scaled: bool = True


def _side_cast_specs(side, nsteps, step_of):
    depth, k, n = side.w.shape
    nblk = max(b for b in range(1, nsteps + 1) if k % b == 0 and (k // b) % 16 == 0)
    bk = k // nblk

    def blk(*g):
        return jnp.minimum(step_of(*g), nblk - 1)

    in_specs = [pl.BlockSpec((None, bk, n), lambda *g: (side.layer, blk(*g), 0)),
                pl.BlockSpec((None, bk, 1), lambda *g: (side.layer, blk(*g), 0))]
    return in_specs, pl.BlockSpec((bk, n), lambda *g: (blk(*g), 0)), jax.ShapeDtypeStruct((k, n), BF16)


def _cast_block(w_ref, g_ref, o_ref, scaled=True):
    w = w_ref[...]
    o_ref[...] = (w * g_ref[...] if scaled else w).astype(o_ref.dtype)


def _rstd(ss_ref, d):
    return lax.rsqrt(ss_ref[...] * (1.0 / d) + RMS_EPS)


def _col_parts(width):
    parts = EPILOGUE_PARTS if width % (EPILOGUE_PARTS * MXU_COLUMNS) == 0 else 1
    step = width // parts
    return [slice(c * step, (c + 1) * step) for c in range(parts)]


def _emit_stream(pieces, h_ref, hb_ref, ss_ref, j):
    row_ss = 0.0
    for cols, h in pieces:
        h_ref[:, cols] = h
        hb_ref[:, cols] = h.astype(BF16)
        row_ss = row_ss + jnp.sum(h * h, axis=1, keepdims=True)

    @pl.when(j == 0)
    def _():
        ss_ref[...] = row_ss

    @pl.when(j > 0)
    def _():
        ss_ref[...] += row_ss


def _stream_out(t, n, tm, tn):
    specs = [pl.BlockSpec((tm, tn), lambda i, j, *kk: (i, j)),
             pl.BlockSpec((tm, tn), lambda i, j, *kk: (i, j)),
             pl.BlockSpec((tm, 1), lambda i, j, *kk: (i, 0))]
    shapes = [jax.ShapeDtypeStruct((t, n), F32), jax.ShapeDtypeStruct((t, n), BF16),
              jax.ShapeDtypeStruct((t, 1), F32)]
    return specs, shapes


def _to_bf16(w, gain=None, layer=None):
    depth, k, n = w.shape
    bk = max(16, min(k, CAST_BLOCK_BYTES // (4 * n)))
    assert k % bk == 0, (k, bk)
    if gain is None:
        gain = jnp.ones((depth, k), F32)
    if layer is None:
        grid, first = (depth, k // bk), 0
        out_spec = pl.BlockSpec((None, bk, n), lambda a, r: (a, r, 0))
        out_shape = jax.ShapeDtypeStruct(w.shape, BF16)
    else:
        grid, first = (1, k // bk), layer
        out_spec = pl.BlockSpec((bk, n), lambda a, r: (r, 0))
        out_shape = jax.ShapeDtypeStruct((k, n), BF16)
    return pl.pallas_call(
        _cast_block,
        grid=grid,
        in_specs=[pl.BlockSpec((None, bk, n), lambda a, r: (a + first, r, 0)),
                  pl.BlockSpec((None, bk, 1), lambda a, r: (a + first, r, 0))],
        out_specs=out_spec,
        out_shape=out_shape,
        compiler_params=_params("parallel", "parallel"),
        name="cast_bf16",
    )(w, gain.reshape(depth, k, 1))


def _enter_kernel(x_ref, xb_ref, ss_ref):
    x = x_ref[...]
    xb_ref[...] = x.astype(BF16)
    ss_ref[...] = jnp.sum(x * x, axis=1, keepdims=True)


def _enter(x):
    t, d = x.shape
    tm = _tile(t, 256)
    return pl.pallas_call(
        _enter_kernel,
        grid=(t // tm,),
        in_specs=[pl.BlockSpec((tm, d), lambda i: (i, 0))],
        out_specs=[pl.BlockSpec((tm, d), lambda i: (i, 0)), pl.BlockSpec((tm, 1), lambda i: (i, 0))],
        out_shape=[jax.ShapeDtypeStruct((t, d), BF16), jax.ShapeDtypeStruct((t, 1), F32)],
        compiler_params=_params("parallel"),
        name="enter",
    )(x)


def _rmsnorm_kernel(x_ref, g_ref, o_ref):
    x = x_ref[...]
    ms = jnp.mean(x * x, axis=-1, keepdims=True)
    o_ref[...] = (x * lax.rsqrt(ms + RMS_EPS) * g_ref[...]).astype(o_ref.dtype)


def _rmsnorm(x, g, out_dtype):
    t, d = x.shape
    tm = _tile(t, 256)
    return pl.pallas_call(
        _rmsnorm_kernel,
        grid=(t // tm,),
        in_specs=[pl.BlockSpec((tm, d), lambda i: (i, 0)),
                  pl.BlockSpec((1, d), lambda i: (0, 0))],
        out_specs=pl.BlockSpec((tm, d), lambda i: (i, 0)),
        out_shape=jax.ShapeDtypeStruct((t, d), out_dtype),
        compiler_params=_params("parallel"),
        name="rmsnorm",
    )(x, g.reshape(1, d))


def _normed_mm_kernel(hb_ref, w_ref, ss_ref, *rest, d, side, scaled):
    if side:
        sw_ref, sg_ref, o_ref, so_ref = rest
        _cast_block(sw_ref, sg_ref, so_ref, scaled)
    else:
        (o_ref,) = rest
    rstd = _rstd(ss_ref, d)
    for c in _col_parts(o_ref.shape[1]):
        acc = _dot(hb_ref[...], w_ref[:, c]) * rstd
        o_ref[:, c] = jnp.square(jnp.maximum(acc, 0.0)).astype(o_ref.dtype)


def _normed_relu2_matmul(hb, ss, w, layer, *, name, side=None):
    t, k = hb.shape
    ncols = w.shape[-1]
    tm, tn = _tile(t, 1024), _tile(ncols, 1024)
    nj = ncols // tn
    in_specs = [pl.BlockSpec((tm, k), lambda i, j: (i, 0)),
                _wspec(w, layer, k, tn),
                pl.BlockSpec((tm, 1), lambda i, j: (i, 0))]
    args = [hb, w, ss]
    out_specs = [pl.BlockSpec((tm, tn), lambda i, j: (i, j))]
    out_shape = [jax.ShapeDtypeStruct((t, ncols), BF16)]
    if side is not None:
        s_in, s_out, s_shape = _side_cast_specs(side, (t // tm) * nj, lambda i, j: i * nj + j)
        in_specs += s_in
        args += [side.w, side.gain.reshape(*side.gain.shape, 1)]
        out_specs.append(s_out)
        out_shape.append(s_shape)
    outs = pl.pallas_call(
        functools.partial(_normed_mm_kernel, d=k, side=side is not None,
                          scaled=side is not None and side.scaled),
        grid=(t // tm, nj),
        in_specs=in_specs,
        out_specs=out_specs,
        out_shape=out_shape,
        compiler_params=_params("parallel", "parallel"),
        name=name,
    )(*args)
    return outs if side is not None else outs[0]


def _proj_kernel(hb_ref, w_ref, ss_ref, qkv_ref, gates_ref, *, d, nq, nqkv):
    j = pl.program_id(1)

    @pl.when(j < nqkv)
    def _():
        scale = _rstd(ss_ref, d) * jnp.where(j < nq, ATTN_Q_SCALE, 1.0)
        for c in _col_parts(qkv_ref.shape[1]):
            qkv_ref[:, c] = (_dot(hb_ref[...], w_ref[:, c]) * scale).astype(qkv_ref.dtype)

    @pl.when(j >= nqkv)
    def _():
        rstd = _rstd(ss_ref, d)
        for c in _col_parts(gates_ref.shape[1]):
            gates_ref[:, c] = jax.nn.sigmoid(_dot(hb_ref[...], w_ref[:, c]) * rstd).astype(gates_ref.dtype)


def _proj(hb, ss, w, layer, *, attn_w, ssm_w, d_model):
    t, k = hb.shape
    tm = _tile(t, 1024)
    tn = next(c for c in (1024, 512, 256, 128)
              if attn_w % c == 0 and ssm_w % c == 0 and (2 * d_model) % c == 0)
    nq, nqkv, nu, ng = attn_w // tn, 3 * attn_w // tn, ssm_w // tn, 2 * d_model // tn

    def wcol(j):
        return jnp.where(j < nqkv, j, j + nu)

    if w.ndim == 2:
        w_spec = pl.BlockSpec((k, tn), lambda i, j: (0, wcol(j)))
    else:
        w_spec = pl.BlockSpec((None, k, tn), lambda i, j: (layer, 0, wcol(j)))
    return pl.pallas_call(
        functools.partial(_proj_kernel, d=k, nq=nq, nqkv=nqkv),
        grid=(t // tm, nqkv + ng),
        in_specs=[pl.BlockSpec((tm, k), lambda i, j: (i, 0)),
                  w_spec,
                  pl.BlockSpec((tm, 1), lambda i, j: (i, 0))],
        out_specs=[pl.BlockSpec((tm, tn), lambda i, j: (i, jnp.minimum(j, nqkv - 1))),
                   pl.BlockSpec((tm, tn), lambda i, j: (i, jnp.maximum(j - nqkv, 0)))],
        out_shape=[jax.ShapeDtypeStruct((t, 3 * attn_w), BF16),
                   jax.ShapeDtypeStruct((t, 2 * d_model), BF16)],
        compiler_params=_params("parallel", "arbitrary"),
        name="proj",
    )(hb, w, ss)


def _proj_u_kernel(hb_ref, w_ref, ss_ref, o_ref, stage_ref, *, d, sub_len):
    acc = _dot(hb_ref[...], w_ref[...]) * _rstd(ss_ref, d)
    for c in range(acc.shape[1] // LANES):
        for r in range(SUBLANES):
            stage_ref[c, pl.ds(r, sub_len, stride=SUBLANES), :] = (
                acc[r * sub_len:(r + 1) * sub_len, c * LANES:(c + 1) * LANES])
    for c in range(acc.shape[1] // LANES):
        o_ref[:, c * LANES:(c + 1) * LANES] = stage_ref[c].astype(o_ref.dtype)


def _proj_u(hb, ss, w, layer, *, col0, width, batch, seq):
    t, k = hb.shape
    sub_len = seq // SUBLANES
    tn = _tile(width, 256)
    assert col0 % tn == 0
    joff = col0 // tn
    if w.ndim == 2:
        w_spec = pl.BlockSpec((k, tn), lambda b, j: (0, j + joff))
    else:
        w_spec = pl.BlockSpec((None, k, tn), lambda b, j: (layer, 0, j + joff))
    return pl.pallas_call(
        functools.partial(_proj_u_kernel, d=k, sub_len=sub_len),
        grid=(batch, width // tn),
        in_specs=[pl.BlockSpec((seq, k), lambda b, j: (b, 0)),
                  w_spec,
                  pl.BlockSpec((seq, 1), lambda b, j: (b, 0))],
        out_specs=pl.BlockSpec((seq, tn), lambda b, j: (b, j)),
        out_shape=jax.ShapeDtypeStruct((t, width), BF16),
        scratch_shapes=[pltpu.VMEM((tn // LANES, seq, LANES), F32)],
        compiler_params=_params("parallel", "parallel"),
        name="proj_u",
    )(hb, w, ss)


def _glu_kernel(y_ref, w_ref, o_ref, stage_ref, *, width):
    y = y_ref[...]
    res = y.astype(F32) * jax.nn.sigmoid(_dot(y, w_ref[...]))
    ntile = width // LANES
    sub_rows = res.shape[0] // SUBLANES
    for c in range(ntile):
        stage_ref[c] = res[:, c * LANES:(c + 1) * LANES]
    for r in range(SUBLANES):
        for c in range(ntile):
            lane0 = r * width + c * LANES
            o_ref[:, lane0:lane0 + LANES] = (
                stage_ref[c, pl.ds(r, sub_rows, stride=SUBLANES), :].astype(o_ref.dtype))


def _glu(y_perm, w, layer, *, batch, seq):
    t, width = y_perm.shape
    assert w.shape[-1] == width and w.shape[-2] == width
    tm = _tile(seq, 512)
    tiles = seq // tm
    return pl.pallas_call(
        functools.partial(_glu_kernel, width=width),
        grid=(t // tm,),
        in_specs=[pl.BlockSpec((tm, width), lambda i: (i, 0)),
                  (pl.BlockSpec((width, width), lambda i: (0, 0)) if w.ndim == 2
                   else pl.BlockSpec((None, width, width), lambda i: (layer, 0, 0)))],
        out_specs=pl.BlockSpec((None, tm // SUBLANES, SUBLANES * width), lambda i: (i // tiles, i % tiles, 0)),
        out_shape=jax.ShapeDtypeStruct((batch, seq // SUBLANES, SUBLANES * width), BF16),
        scratch_shapes=[pltpu.VMEM((width // LANES, tm, LANES), F32)],
        compiler_params=_params("parallel"),
        name="glu",
    )(y_perm, w)


def _merge_kernel(a_ref, wa_ref, s_ref, ws_ref, ga_ref, gs_ref, o_ref, *, nr, sub_len, width):
    for c in _col_parts(o_ref.shape[1]):
        up_a = _dot(a_ref[...], wa_ref[:, c])
        for rl in range(nr):
            rows = slice(rl * sub_len, (rl + 1) * sub_len)
            up_s = _dot(s_ref[:, rl * width:(rl + 1) * width], ws_ref[:, c])
            o_ref[rows, c] = (ga_ref[rows, c].astype(F32) * up_a[rows, :]
                              + gs_ref[rows, c].astype(F32) * up_s).astype(o_ref.dtype)


def _merge(attn, wa, ssm3, ws, gates, layer, *, seq):
    t, ka = attn.shape
    sub_len = seq // SUBLANES
    assert ssm3.shape[1] == sub_len
    ks = ssm3.shape[2] // SUBLANES
    d = wa.shape[-1]
    tm, tn = _tile(seq, 1024), _tile(d, 1024)
    assert tm % sub_len == 0
    nr = tm // sub_len
    tiles = seq // tm
    nj = d // tn
    return pl.pallas_call(
        functools.partial(_merge_kernel, nr=nr, sub_len=sub_len, width=ks),
        grid=(t // tm, nj),
        in_specs=[pl.BlockSpec((tm, ka), lambda i, j: (i, 0)),
                  _wspec(wa, layer, ka, tn),
                  pl.BlockSpec((None, sub_len, nr * ks), lambda i, j: (i // tiles, 0, i % tiles)),
                  _wspec(ws, layer, ks, tn),
                  pl.BlockSpec((tm, tn), lambda i, j: (i, j)),
                  pl.BlockSpec((tm, tn), lambda i, j: (i, j + nj))],
        out_specs=pl.BlockSpec((tm, tn), lambda i, j: (i, j)),
        out_shape=jax.ShapeDtypeStruct((t, d), BF16),
        compiler_params=_params("parallel", "parallel"),
        name="merge",
    )(attn, wa, ssm3, ws, gates, gates)


def _resid_mm_kernel(x_ref, w_ref, h_ref, *rest, nk, side, scaled):
    if side:
        sw_ref, sg_ref, o_ref, ob_ref, ss_ref, so_ref = rest
        _cast_block(sw_ref, sg_ref, so_ref, scaled)
    else:
        o_ref, ob_ref, ss_ref = rest
    j = pl.program_id(1)
    parts = _col_parts(o_ref.shape[1])
    if nk == 1:
        _emit_stream([(c, h_ref[:, c] + _dot(x_ref[...], w_ref[:, c])) for c in parts],
                     o_ref, ob_ref, ss_ref, j)
        return
    k = pl.program_id(2)

    @pl.when(k == 0)
    def _():
        for c in parts:
            o_ref[:, c] = h_ref[:, c] + _dot(x_ref[...], w_ref[:, c])

    @pl.when(jnp.logical_and(k > 0, k < nk - 1))
    def _():
        for c in parts:
            o_ref[:, c] += _dot(x_ref[...], w_ref[:, c])

    @pl.when(k == nk - 1)
    def _():
        _emit_stream([(c, o_ref[:, c] + _dot(x_ref[...], w_ref[:, c])) for c in parts],
                     o_ref, ob_ref, ss_ref, j)


def _resid_matmul(x, w, layer, h, *, name, side=None):
    t, k = x.shape
    n = w.shape[-1]
    tk = k if k <= 4096 else _tile(k, 2048)
    nk = k // tk
    assert nk == 1 or nk >= 2
    tm, tn = _tile(t, 1024), _tile(n, 512 if nk == 1 else 1024)
    nj = n // tn
    in_specs = [pl.BlockSpec((tm, tk), lambda i, j, kk: (i, kk)),
                _wspec(w, layer, tk, tn),
                pl.BlockSpec((tm, tn), lambda i, j, kk: (i, j))]
    args = [x, w, h]
    out_specs, out_shape = _stream_out(t, n, tm, tn)
    if side is not None:
        s_in, s_out, s_shape = _side_cast_specs(side, (t // tm) * nj * nk,
                                                lambda i, j, kk: (i * nj + j) * nk + kk)
        in_specs += s_in
        args += [side.w, side.gain.reshape(*side.gain.shape, 1)]
        out_specs.append(s_out)
        out_shape.append(s_shape)
    return pl.pallas_call(
        functools.partial(_resid_mm_kernel, nk=nk, side=side is not None,
                          scaled=side is not None and side.scaled),
        grid=(t // tm, nj, nk),
        in_specs=in_specs,
        out_specs=out_specs,
        out_shape=out_shape,
        compiler_params=_params("parallel", "arbitrary", "arbitrary"),
        name=name,
    )(*args)


def _ple_kernel(hb_ref, wg_ref, ss_ref, p_ref, wp_ref, h_ref, o_ref, ob_ref, sso_ref, *, d):
    half = o_ref.shape[0] // 2
    sums = []
    for r in (slice(0, half), slice(half, 2 * half)):
        z = _dot(hb_ref[r, :], wg_ref[...]) * _rstd(ss_ref, d)[r, :]
        h = h_ref[r, :] + _dot(p_ref[r, :].astype(BF16), wp_ref[...]) * jax.nn.sigmoid(z)
        o_ref[r, :] = h
        ob_ref[r, :] = h.astype(BF16)
        sums.append(jnp.sum(h * h, axis=1, keepdims=True))
    row_ss = jnp.concatenate(sums, axis=0)
    j = pl.program_id(1)

    @pl.when(j == 0)
    def _():
        sso_ref[...] = row_ss

    @pl.when(j > 0)
    def _():
        sso_ref[...] += row_ss


def _ple(hb, ss, wg, p, wp, layer, h):
    t, k = hb.shape
    kp = p.shape[1]
    n = wg.shape[-1]
    tm, tn = _tile(t, 1024), _tile(n, 512)
    out_specs, out_shape = _stream_out(t, n, tm, tn)
    return pl.pallas_call(
        functools.partial(_ple_kernel, d=k),
        grid=(t // tm, n // tn),
        in_specs=[pl.BlockSpec((tm, k), lambda i, j: (i, 0)),
                  _wspec(wg, layer, k, tn),
                  pl.BlockSpec((tm, 1), lambda i, j: (i, 0)),
                  pl.BlockSpec((tm, kp), lambda i, j: (i, 0)),
                  _wspec(wp, layer, kp, tn),
                  pl.BlockSpec((tm, tn), lambda i, j: (i, j))],
        out_specs=out_specs,
        out_shape=out_shape,
        compiler_params=_params("parallel", "arbitrary"),
        name="ple",
    )(hb, wg, ss, p, wp, h)


def _attn_kernel(q_ref, k_ref, v_ref, tri_ref, *rest, tq, nheads, scaled):
    nsides = len(scaled)
    side_in, (o_ref, *side_out), (acc_ref, carry_ref) = rest[:2 * nsides], rest[2 * nsides:3 * nsides + 1], rest[3 * nsides + 1:]
    for n in range(nsides):
        _cast_block(side_in[2 * n], side_in[2 * n + 1], side_out[n], scaled[n])
    qi = pl.program_id(2)
    tri = tri_ref[...]
    row = lax.broadcasted_iota(jnp.int32, (tq, tq), 0)
    col = lax.broadcasted_iota(jnp.int32, (tq, tq), 1)
    causal = col < row

    def key_block(kb, masked):
        start = pl.multiple_of(kb * tq, tq)
        heads = range(nheads)
        lanes = [slice(hd * HEAD_DIM, (hd + 1) * HEAD_DIM) for hd in heads]
        zs = [lax.dot_general(q_ref[:, lanes[hd]], k_ref[pl.ds(start, tq), lanes[hd]],
                              (((1,), (1,)), ((), ())), preferred_element_type=F32) for hd in heads]
        tails = []
        for z in zs:
            neg_abs = lax.bitcast_convert_type(
                lax.bitcast_convert_type(z, jnp.uint32) | jnp.uint32(0x80000000), F32)
            softplus = jnp.maximum(z, 0.0) + jnp.log2(1.0 + jnp.exp2(neg_abs))
            if masked:
                softplus = jnp.where(causal, softplus, 0.0)
            hi = softplus.astype(BF16)
            lo = (softplus - hi.astype(F32)).astype(BF16)
            tails.append(_dot(jnp.concatenate([hi, lo], axis=1), tri))
        for hd in heads:
            w = jnp.exp2(zs[hd] + tails[hd] + carry_ref[hd])
            if masked:
                w = jnp.where(causal, w, 0.0)
            acc_ref[:, lanes[hd]] += _dot(w.astype(BF16), v_ref[pl.ds(start, tq), lanes[hd]])
            carry_ref[hd] += tails[hd][:, 0:1]

    acc_ref[...] = jnp.zeros_like(acc_ref)
    carry_ref[...] = jnp.zeros_like(carry_ref)
    key_block(qi, True)

    def body(i, c):
        key_block(qi - 1 - i, False)
        return c

    lax.fori_loop(0, qi, body, 0)
    o_ref[...] = acc_ref[...].astype(o_ref.dtype)


def _attention(qkv, sides, *, batch, seq, heads):
    tq = _tile(seq, 256)
    nq = seq // tq
    nheads = _tile(heads, ATTN_HEADS_PER_STEP)
    hgroups = heads // nheads
    width = nheads * HEAD_DIM
    r = jnp.arange(tq)
    tri = -(r[:, None] >= r[None, :]).astype(BF16)
    tri2 = jnp.concatenate([tri, tri], axis=0)
    nsteps = batch * hgroups * nq
    s_in, s_args, s_out, s_shape = [], [], [], []
    for side in sides:
        specs, out_spec, out_shape = _side_cast_specs(side, nsteps, lambda b, h, i: (b * hgroups + h) * nq + i)
        s_in += specs
        s_args += [side.w, side.gain.reshape(*side.gain.shape, 1)]
        s_out.append(out_spec)
        s_shape.append(out_shape)
    out, *cast = pl.pallas_call(
        functools.partial(_attn_kernel, tq=tq, nheads=nheads, scaled=tuple(sd.scaled for sd in sides)),
        grid=(batch, hgroups, nq),
        in_specs=[pl.BlockSpec((tq, width), lambda b, h, i: (b * nq + i, h)),
                  pl.BlockSpec((seq, width), lambda b, h, i: (b, hgroups + h)),
                  pl.BlockSpec((seq, width), lambda b, h, i: (b, 2 * hgroups + h)),
                  pl.BlockSpec((2 * tq, tq), lambda b, h, i: (0, 0))] + s_in,
        out_specs=[pl.BlockSpec((tq, width), lambda b, h, i: (b * nq + i, h))] + s_out,
        out_shape=[jax.ShapeDtypeStruct((batch * seq, heads * HEAD_DIM), BF16)] + s_shape,
        scratch_shapes=[pltpu.VMEM((tq, width), F32), pltpu.VMEM((nheads, tq, 1), F32)],
        compiler_params=_params("arbitrary", "arbitrary", "arbitrary"),
        name="stickbreak_attn",
    )(qkv, qkv, qkv, tri2, *s_args)
    return out, cast


def _ssm_kernel(u_ref, b_ref, m_ref, c_ref, d_ref, o_ref, x_ref, xb_ref, *, seq):
    ns = SLAB_STATES
    nchunks = seq // SSM_CHUNK
    steps = SSM_CHUNK // SUBLANES

    def rows(k):
        return slice(k * SSM_CHUNK, (k + 1) * SSM_CHUNK)

    def group(i):
        return slice(i * SUBLANES, (i + 1) * SUBLANES)

    def project(k):
        x_ref[rows(k), :] = _dot(u_ref[rows(k), :], b_ref[0])

    def scan(k, v):
        v_re, v_im = v
        for s in range(steps):
            r = group(k * steps + s)
            l_re = m_ref[0, 0]
            l_im = m_ref[0, 1]
            v_re, v_im = (l_re * v_re - l_im * v_im + x_ref[r, 0:ns],
                          l_re * v_im + l_im * v_re + x_ref[r, ns:2 * ns])
            x_ref[r, 0:ns] = v_re
            x_ref[r, ns:2 * ns] = v_im
        return v_re, v_im

    def carry_in(e):
        e_re, e_im = e
        for n, shift in enumerate((1, 2, 4)):
            a_re = m_ref[0, 2 + 2 * n]
            a_im = m_ref[0, 3 + 2 * n]
            s_re = pltpu.roll(e_re, shift, 0)
            s_im = pltpu.roll(e_im, shift, 0)
            e_re, e_im = (e_re + (a_re * s_re - a_im * s_im),
                          e_im + (a_re * s_im + a_im * s_re))
        not_first = m_ref[0, 8]
        return not_first * pltpu.roll(e_re, 1, 0), not_first * pltpu.roll(e_im, 1, 0)

    def fix(k, w):
        w_re, w_im = w
        for s in range(0, steps, 2):
            parts = []
            for i in (k * steps + s, k * steps + s + 1):
                l_re = m_ref[0, 0]
                l_im = m_ref[0, 1]
                w_re, w_im = l_re * w_re - l_im * w_im, l_re * w_im + l_im * w_re
                parts.append(jnp.concatenate([x_ref[group(i), 0:ns] + w_re,
                                              x_ref[group(i), ns:2 * ns] + w_im], axis=1))
            i0 = k * steps + s
            xb_ref[i0 * SUBLANES:(i0 + 2) * SUBLANES, :] = jnp.concatenate(parts, axis=0).astype(BF16)
        return w_re, w_im

    def readout(k):
        y = _dot(xb_ref[rows(k), :], c_ref[0]) + d_ref[...] * u_ref[rows(k), :].astype(F32)
        o_ref[rows(k), :] = jax.nn.gelu(y, approximate=True).astype(o_ref.dtype)

    project(0)
    if nchunks > 1:
        project(1)
    zero = jnp.zeros((SUBLANES, ns), F32)
    v = scan(0, (zero, zero))
    for k in range(1, nchunks):
        if k + 1 < nchunks:
            project(k + 1)
        v = scan(k, v)
    w = fix(0, carry_in(v))
    for k in range(1, nchunks):
        w = fix(k, w)
        readout(k - 1)
    readout(nchunks - 1)


def _ssm_operands(lam_re, lam_im, log_dt, b_re, b_im, c_re, c_im, sub_len):
    g = lam_re.shape[0]
    nslab = g // SLAB_GROUPS
    dt = jnp.exp(log_dt)[:, None]
    lam = lax.complex(lam_re, lam_im)
    lam_bar = jnp.exp(lam * dt)
    b_bar = ((lam_bar - 1.0) / lam)[..., None] * lax.complex(b_re, b_im)

    def block_diag(rows, per_row_group, width, per_col_group):
        n = rows.shape[-1]
        q = jnp.arange(width)
        tiled = jnp.dot(rows, (q[None, :] % n == jnp.arange(n)[:, None]).astype(F32))
        same = (jnp.arange(rows.shape[1]) // per_row_group)[:, None] == (q // per_col_group)[None, :]
        return jnp.where(same, tiled, 0.0)

    def b_block(part):
        t = part.reshape(nslab, SLAB_GROUPS, SSM_STATE, SSM_GROUP).transpose(0, 1, 3, 2)
        return block_diag(t.reshape(nslab, SLAB, SSM_STATE), SSM_GROUP, SLAB_STATES, SSM_STATE)

    def c_block(part):
        t = part.reshape(nslab, SLAB_GROUPS, SSM_GROUP, SSM_STATE).transpose(0, 1, 3, 2)
        return block_diag(t.reshape(nslab, SLAB_STATES, SSM_GROUP), SSM_STATE, SLAB, SSM_GROUP)

    b_mat = jnp.concatenate([b_block(b_bar.real), b_block(b_bar.imag)], axis=2).astype(BF16)
    c_mat = jnp.concatenate([c_block(c_re), c_block(-c_im)], axis=1).astype(BF16)

    lam1 = lam_bar.reshape(nslab, 1, SLAB_STATES)
    big1, sq, n = jnp.ones_like(lam1), lam1, sub_len
    while n:
        if n & 1:
            big1 = big1 * sq
        sq, n = sq * sq, n >> 1
    big2 = big1 * big1
    big4 = big2 * big2
    rows = jnp.arange(SUBLANES)[None, :, None]
    every = jnp.broadcast_to(lam1, (nslab, SUBLANES, SLAB_STATES))
    planes = [every.real, every.imag]
    for big, shift in ((big1, 1), (big2, 2), (big4, 4)):
        a = jnp.where(rows >= shift, big, 0.0)
        planes += [a.real, a.imag]
    planes.append(jnp.broadcast_to((rows >= 1).astype(F32), (nslab, SUBLANES, SLAB_STATES)))
    mult = jnp.stack(planes, axis=1).astype(F32)
    return b_mat, mult, c_mat


def _ssm(u_perm, b_mat, mult, c_mat, d_skip, layer, *, batch, seq):
    nslab = b_mat.shape[1]
    width = nslab * SLAB
    sub_len = seq // SUBLANES
    assert seq % SSM_CHUNK == 0 and (SSM_CHUNK // SUBLANES) % 2 == 0
    return pl.pallas_call(
        functools.partial(_ssm_kernel, seq=seq),
        grid=(batch, nslab),
        in_specs=[pl.BlockSpec((seq, SLAB), lambda b, s: (b, s)),
                  pl.BlockSpec((None, 1, SLAB, 2 * SLAB_STATES), lambda b, s: (layer, s, 0, 0)),
                  pl.BlockSpec((None, 1, 9, SUBLANES, SLAB_STATES), lambda b, s: (layer, s, 0, 0, 0)),
                  pl.BlockSpec((None, 1, 2 * SLAB_STATES, SLAB), lambda b, s: (layer, s, 0, 0)),
                  pl.BlockSpec((None, 1, SLAB), lambda b, s: (layer, 0, s))],
        out_specs=pl.BlockSpec((seq, SLAB), lambda b, s: (b, s)),
        out_shape=jax.ShapeDtypeStruct((batch * seq, width), BF16),
        scratch_shapes=[pltpu.VMEM((seq, 2 * SLAB_STATES), F32), pltpu.VMEM((seq, 2 * SLAB_STATES), BF16)],
        compiler_params=_params("parallel", "parallel"),
        name="s5_ssm",
    )(u_perm, b_mat, mult, c_mat, d_skip.reshape(-1, 1, width))


def kernel(x, p, g_mix, w_in, w_br_attn, lam_re, lam_im, log_dt, b_re, b_im, c_re, c_im, d_skip, w_glu, w_br_ssm, w_o, g_mlp, w_ff1, w_ff2, g_ple, w_ple_gate, w_ple, g_final):
    bsz, seq, d = x.shape
    depth = w_in.shape[0]
    attn_w = w_br_attn.shape[1]
    ssm_w = w_br_ssm.shape[1]
    heads = attn_w // HEAD_DIM
    t = bsz * seq
    assert w_in.shape[2] == 3 * attn_w + ssm_w + 2 * d

    wb_in = _to_bf16(w_in, g_mix, layer=0)
    wb_ple = _to_bf16(w_ple)
    b_mat, mult, c_mat = jax.vmap(functools.partial(_ssm_operands, sub_len=seq // SUBLANES))(
        lam_re, lam_im, log_dt, b_re, b_im, c_re, c_im)

    def plain(w):
        return jnp.ones(w.shape[:2], F32)

    h = x.reshape(t, d)
    hb, ss = _enter(h)
    for i in range(depth):
        qkv, gates = _proj(hb, ss, wb_in, i, attn_w=attn_w, ssm_w=ssm_w, d_model=d)
        u_perm = _proj_u(hb, ss, wb_in, i, col0=3 * attn_w, width=ssm_w, batch=bsz, seq=seq)
        attn, (wb_ff1, wb_o, wb_ple_gate, wb_br_attn, wb_br_ssm, wb_glu) = _attention(
            qkv,
            [_SideCast(w_ff1, g_mlp, i), _SideCast(w_o, plain(w_o), i, False), _SideCast(w_ple_gate, g_ple, i),
             _SideCast(w_br_attn, plain(w_br_attn), i, False), _SideCast(w_br_ssm, plain(w_br_ssm), i, False),
             _SideCast(w_glu, plain(w_glu), i, False)],
            batch=bsz, seq=seq, heads=heads)
        y_perm = _ssm(u_perm, b_mat, mult, c_mat, d_skip, i, batch=bsz, seq=seq)
        ssm3 = _glu(y_perm, wb_glu, i, batch=bsz, seq=seq)
        merged = _merge(attn, wb_br_attn, ssm3, wb_br_ssm, gates, i, seq=seq)
        h, hb, ss = _resid_matmul(merged, wb_o, i, h, name="out_proj")

        act, wb_ff2 = _normed_relu2_matmul(hb, ss, wb_ff1, i, name="ff1",
                                           side=_SideCast(w_ff2, plain(w_ff2), i, False))
        if i + 1 < depth:
            h, hb, ss, wb_in = _resid_matmul(act, wb_ff2, i, h, name="ff2",
                                             side=_SideCast(w_in, g_mix, i + 1))
        else:
            h, hb, ss = _resid_matmul(act, wb_ff2, i, h, name="ff2")

        h, hb, ss = _ple(hb, ss, wb_ple_gate, p[i].reshape(t, -1), wb_ple, i, h)
    return _rmsnorm(h, g_final, x.dtype).reshape(bsz, seq, d)
```

```python
import functools
import math
from typing import NamedTuple

import jax
import jax.numpy as jnp
from jax import lax
from jax.experimental import pallas as pl
from jax.experimental.pallas import tpu as pltpu

F32 = jnp.float32
BF16 = jnp.bfloat16

RMS_EPS = 1e-6
HEAD_DIM = 128
ATTN_Q_SCALE = HEAD_DIM ** -0.5 * math.log2(math.e)
ATTN_HEADS_PER_STEP = 8
SSM_GROUP = 16
SSM_STATE = 64
SLAB = 256
SLAB_GROUPS = SLAB // SSM_GROUP
SLAB_STATES = SLAB_GROUPS * SSM_STATE
SUBLANES = 8
LANES = 128
SSM_CHUNK = 512
RING_SLOTS = 3
V7X_VMEM_LIMIT = 56 * 1024 * 1024
CAST_BLOCK_BYTES = 8 * 1024 * 1024


def _tile(dim, pref):
    if dim % pref == 0:
        return pref
    assert dim < pref, (dim, pref)
    return dim


def _params(*sem):
    return pltpu.CompilerParams(dimension_semantics=sem, vmem_limit_bytes=V7X_VMEM_LIMIT)


def _dot(a, b):
    return jnp.dot(a, b, preferred_element_type=F32)


def _wspec(w, layer, k, tn, joff=0):
    if w.ndim == 2:
        return pl.BlockSpec((k, tn), lambda i, j, *kk: (kk[0] if kk else 0, j + joff))
    return pl.BlockSpec((None, k, tn), lambda i, j, *kk: (layer, kk[0] if kk else 0, j + joff))


class _SideCast(NamedTuple):
    w: jax.Array
    gain: jax.Array
    layer: int


def _side_cast_specs(side, nsteps, step_of):
    depth, k, n = side.w.shape
    nblk = max(b for b in range(1, nsteps + 1) if k % b == 0 and (k // b) % 16 == 0)
    bk = k // nblk

    def blk(*g):
        return jnp.minimum(step_of(*g), nblk - 1)

    in_specs = [pl.BlockSpec((None, bk, n), lambda *g: (side.layer, blk(*g), 0)),
                pl.BlockSpec((None, bk, 1), lambda *g: (side.layer, blk(*g), 0))]
    return in_specs, pl.BlockSpec((bk, n), lambda *g: (blk(*g), 0)), jax.ShapeDtypeStruct((k, n), BF16)


def _cast_block(w_ref, g_ref, o_ref):
    o_ref[...] = (w_ref[...] * g_ref[...]).astype(o_ref.dtype)


def _rstd(ss_ref, d):
    return lax.rsqrt(ss_ref[...] * (1.0 / d) + RMS_EPS)


def _emit_stream(h, h_ref, hb_ref, ss_ref, j):
    h_ref[...] = h
    hb_ref[...] = h.astype(BF16)
    row_ss = jnp.sum(h * h, axis=1, keepdims=True)

    @pl.when(j == 0)
    def _():
        ss_ref[...] = row_ss

    @pl.when(j > 0)
    def _():
        ss_ref[...] += row_ss


def _stream_out(t, n, tm, tn):
    specs = [pl.BlockSpec((tm, tn), lambda i, j, *kk: (i, j)),
             pl.BlockSpec((tm, tn), lambda i, j, *kk: (i, j)),
             pl.BlockSpec((tm, 1), lambda i, j, *kk: (i, 0))]
    shapes = [jax.ShapeDtypeStruct((t, n), F32), jax.ShapeDtypeStruct((t, n), BF16),
              jax.ShapeDtypeStruct((t, 1), F32)]
    return specs, shapes


def _to_bf16(w, gain=None, layer=None):
    depth, k, n = w.shape
    bk = max(16, min(k, CAST_BLOCK_BYTES // (4 * n)))
    assert k % bk == 0, (k, bk)
    if gain is None:
        gain = jnp.ones((depth, k), F32)
    if layer is None:
        grid, first = (depth, k // bk), 0
        out_spec = pl.BlockSpec((None, bk, n), lambda a, r: (a, r, 0))
        out_shape = jax.ShapeDtypeStruct(w.shape, BF16)
    else:
        grid, first = (1, k // bk), layer
        out_spec = pl.BlockSpec((bk, n), lambda a, r: (r, 0))
        out_shape = jax.ShapeDtypeStruct((k, n), BF16)
    return pl.pallas_call(
        _cast_block,
        grid=grid,
        in_specs=[pl.BlockSpec((None, bk, n), lambda a, r: (a + first, r, 0)),
                  pl.BlockSpec((None, bk, 1), lambda a, r: (a + first, r, 0))],
        out_specs=out_spec,
        out_shape=out_shape,
        compiler_params=_params("parallel", "parallel"),
        name="cast_bf16",
    )(w, gain.reshape(depth, k, 1))


def _enter_kernel(x_ref, xb_ref, ss_ref):
    x = x_ref[...]
    xb_ref[...] = x.astype(BF16)
    ss_ref[...] = jnp.sum(x * x, axis=1, keepdims=True)


def _enter(x):
    t, d = x.shape
    tm = _tile(t, 256)
    return pl.pallas_call(
        _enter_kernel,
        grid=(t // tm,),
        in_specs=[pl.BlockSpec((tm, d), lambda i: (i, 0))],
        out_specs=[pl.BlockSpec((tm, d), lambda i: (i, 0)), pl.BlockSpec((tm, 1), lambda i: (i, 0))],
        out_shape=[jax.ShapeDtypeStruct((t, d), BF16), jax.ShapeDtypeStruct((t, 1), F32)],
        compiler_params=_params("parallel"),
        name="enter",
    )(x)


def _rmsnorm_kernel(x_ref, g_ref, o_ref):
    x = x_ref[...]
    ms = jnp.mean(x * x, axis=-1, keepdims=True)
    o_ref[...] = (x * lax.rsqrt(ms + RMS_EPS) * g_ref[...]).astype(o_ref.dtype)


def _rmsnorm(x, g, out_dtype):
    t, d = x.shape
    tm = _tile(t, 256)
    return pl.pallas_call(
        _rmsnorm_kernel,
        grid=(t // tm,),
        in_specs=[pl.BlockSpec((tm, d), lambda i: (i, 0)),
                  pl.BlockSpec((1, d), lambda i: (0, 0))],
        out_specs=pl.BlockSpec((tm, d), lambda i: (i, 0)),
        out_shape=jax.ShapeDtypeStruct((t, d), out_dtype),
        compiler_params=_params("parallel"),
        name="rmsnorm",
    )(x, g.reshape(1, d))


def _normed_mm_kernel(hb_ref, w_ref, ss_ref, *rest, d, side):
    if side:
        sw_ref, sg_ref, o_ref, so_ref = rest
        _cast_block(sw_ref, sg_ref, so_ref)
    else:
        (o_ref,) = rest
    acc = _dot(hb_ref[...], w_ref[...]) * _rstd(ss_ref, d)
    o_ref[...] = jnp.square(jnp.maximum(acc, 0.0)).astype(o_ref.dtype)


def _normed_relu2_matmul(hb, ss, w, layer, *, name, side=None):
    t, k = hb.shape
    ncols = w.shape[-1]
    tm, tn = _tile(t, 1024), _tile(ncols, 1024)
    nj = ncols // tn
    in_specs = [pl.BlockSpec((tm, k), lambda i, j: (i, 0)),
                _wspec(w, layer, k, tn),
                pl.BlockSpec((tm, 1), lambda i, j: (i, 0))]
    args = [hb, w, ss]
    out_specs = [pl.BlockSpec((tm, tn), lambda i, j: (i, j))]
    out_shape = [jax.ShapeDtypeStruct((t, ncols), BF16)]
    if side is not None:
        s_in, s_out, s_shape = _side_cast_specs(side, (t // tm) * nj, lambda i, j: i * nj + j)
        in_specs += s_in
        args += [side.w, side.gain.reshape(*side.gain.shape, 1)]
        out_specs.append(s_out)
        out_shape.append(s_shape)
    outs = pl.pallas_call(
        functools.partial(_normed_mm_kernel, d=k, side=side is not None),
        grid=(t // tm, nj),
        in_specs=in_specs,
        out_specs=out_specs,
        out_shape=out_shape,
        compiler_params=_params("parallel", "parallel"),
        name=name,
    )(*args)
    return outs if side is not None else outs[0]


def _proj_kernel(hb_ref, w_ref, ss_ref, qkv_ref, gates_ref, *, d, nq, nqkv):
    j = pl.program_id(1)

    @pl.when(j < nqkv)
    def _():
        scale = jnp.where(j < nq, ATTN_Q_SCALE, 1.0)
        qkv_ref[...] = (_dot(hb_ref[...], w_ref[...]) * (_rstd(ss_ref, d) * scale)).astype(qkv_ref.dtype)

    @pl.when(j >= nqkv)
    def _():
        gates_ref[...] = jax.nn.sigmoid(_dot(hb_ref[...], w_ref[...]) * _rstd(ss_ref, d)).astype(gates_ref.dtype)


def _proj(hb, ss, w, layer, *, attn_w, ssm_w, d_model):
    t, k = hb.shape
    tm = _tile(t, 1024)
    tn = next(c for c in (1024, 512, 256, 128)
              if attn_w % c == 0 and ssm_w % c == 0 and (2 * d_model) % c == 0)
    nq, nqkv, nu, ng = attn_w // tn, 3 * attn_w // tn, ssm_w // tn, 2 * d_model // tn

    def wcol(j):
        return jnp.where(j < nqkv, j, j + nu)

    if w.ndim == 2:
        w_spec = pl.BlockSpec((k, tn), lambda i, j: (0, wcol(j)))
    else:
        w_spec = pl.BlockSpec((None, k, tn), lambda i, j: (layer, 0, wcol(j)))
    return pl.pallas_call(
        functools.partial(_proj_kernel, d=k, nq=nq, nqkv=nqkv),
        grid=(t // tm, nqkv + ng),
        in_specs=[pl.BlockSpec((tm, k), lambda i, j: (i, 0)),
                  w_spec,
                  pl.BlockSpec((tm, 1), lambda i, j: (i, 0))],
        out_specs=[pl.BlockSpec((tm, tn), lambda i, j: (i, jnp.minimum(j, nqkv - 1))),
                   pl.BlockSpec((tm, tn), lambda i, j: (i, jnp.maximum(j - nqkv, 0)))],
        out_shape=[jax.ShapeDtypeStruct((t, 3 * attn_w), BF16),
                   jax.ShapeDtypeStruct((t, 2 * d_model), BF16)],
        compiler_params=_params("parallel", "arbitrary"),
        name="proj",
    )(hb, w, ss)


def _proj_u_kernel(hb_ref, w_ref, ss_ref, o_ref, stage_ref, *, d, sub_len):
    acc = _dot(hb_ref[...], w_ref[...]) * _rstd(ss_ref, d)
    for c in range(acc.shape[1] // LANES):
        for r in range(SUBLANES):
            stage_ref[c, pl.ds(r, sub_len, stride=SUBLANES), :] = (
                acc[r * sub_len:(r + 1) * sub_len, c * LANES:(c + 1) * LANES])
    for c in range(acc.shape[1] // LANES):
        o_ref[:, c * LANES:(c + 1) * LANES] = stage_ref[c].astype(o_ref.dtype)


def _proj_u(hb, ss, w, layer, *, col0, width, batch, seq):
    t, k = hb.shape
    sub_len = seq // SUBLANES
    tn = _tile(width, 256)
    assert col0 % tn == 0
    joff = col0 // tn
    if w.ndim == 2:
        w_spec = pl.BlockSpec((k, tn), lambda b, j: (0, j + joff))
    else:
        w_spec = pl.BlockSpec((None, k, tn), lambda b, j: (layer, 0, j + joff))
    return pl.pallas_call(
        functools.partial(_proj_u_kernel, d=k, sub_len=sub_len),
        grid=(batch, width // tn),
        in_specs=[pl.BlockSpec((seq, k), lambda b, j: (b, 0)),
                  w_spec,
                  pl.BlockSpec((seq, 1), lambda b, j: (b, 0))],
        out_specs=pl.BlockSpec((seq, tn), lambda b, j: (b, j)),
        out_shape=jax.ShapeDtypeStruct((t, width), BF16),
        scratch_shapes=[pltpu.VMEM((tn // LANES, seq, LANES), F32)],
        compiler_params=_params("parallel", "parallel"),
        name="proj_u",
    )(hb, w, ss)


def _glu_kernel(y_ref, w_ref, o_ref, stage_ref, *, width):
    y = y_ref[...]
    res = y.astype(F32) * jax.nn.sigmoid(_dot(y, w_ref[...]))
    ntile = width // LANES
    sub_rows = res.shape[0] // SUBLANES
    for c in range(ntile):
        stage_ref[c] = res[:, c * LANES:(c + 1) * LANES]
    for r in range(SUBLANES):
        for c in range(ntile):
            lane0 = r * width + c * LANES
            o_ref[:, lane0:lane0 + LANES] = (
                stage_ref[c, pl.ds(r, sub_rows, stride=SUBLANES), :].astype(o_ref.dtype))


def _glu(y_perm, w, layer, *, batch, seq):
    t, width = y_perm.shape
    assert w.shape[-1] == width and w.shape[-2] == width
    tm = _tile(seq, 512)
    tiles = seq // tm
    return pl.pallas_call(
        functools.partial(_glu_kernel, width=width),
        grid=(t // tm,),
        in_specs=[pl.BlockSpec((tm, width), lambda i: (i, 0)),
                  (pl.BlockSpec((width, width), lambda i: (0, 0)) if w.ndim == 2
                   else pl.BlockSpec((None, width, width), lambda i: (layer, 0, 0)))],
        out_specs=pl.BlockSpec((None, tm // SUBLANES, SUBLANES * width), lambda i: (i // tiles, i % tiles, 0)),
        out_shape=jax.ShapeDtypeStruct((batch, seq // SUBLANES, SUBLANES * width), BF16),
        scratch_shapes=[pltpu.VMEM((width // LANES, tm, LANES), F32)],
        compiler_params=_params("parallel"),
        name="glu",
    )(y_perm, w)


def _merge_kernel(a_ref, wa_ref, s_ref, ws_ref, ga_ref, gs_ref, o_ref, *, nr, sub_len, width):
    up_a = _dot(a_ref[...], wa_ref[...])
    for rl in range(nr):
        rows = slice(rl * sub_len, (rl + 1) * sub_len)
        up_s = _dot(s_ref[:, rl * width:(rl + 1) * width], ws_ref[...])
        o_ref[rows, :] = (ga_ref[rows, :].astype(F32) * up_a[rows, :]
                          + gs_ref[rows, :].astype(F32) * up_s).astype(o_ref.dtype)


def _merge(attn, wa, ssm3, ws, gates, layer, *, seq):
    t, ka = attn.shape
    sub_len = seq // SUBLANES
    assert ssm3.shape[1] == sub_len
    ks = ssm3.shape[2] // SUBLANES
    d = wa.shape[-1]
    tm, tn = _tile(seq, 1024), _tile(d, 1024)
    assert tm % sub_len == 0
    nr = tm // sub_len
    tiles = seq // tm
    nj = d // tn
    return pl.pallas_call(
        functools.partial(_merge_kernel, nr=nr, sub_len=sub_len, width=ks),
        grid=(t // tm, nj),
        in_specs=[pl.BlockSpec((tm, ka), lambda i, j: (i, 0)),
                  _wspec(wa, layer, ka, tn),
                  pl.BlockSpec((None, sub_len, nr * ks), lambda i, j: (i // tiles, 0, i % tiles)),
                  _wspec(ws, layer, ks, tn),
                  pl.BlockSpec((tm, tn), lambda i, j: (i, j)),
                  pl.BlockSpec((tm, tn), lambda i, j: (i, j + nj))],
        out_specs=pl.BlockSpec((tm, tn), lambda i, j: (i, j)),
        out_shape=jax.ShapeDtypeStruct((t, d), BF16),
        compiler_params=_params("parallel", "parallel"),
        name="merge",
    )(attn, wa, ssm3, ws, gates, gates)


def _resid_mm_kernel(x_ref, w_ref, h_ref, *rest, nk, side):
    if side:
        sw_ref, sg_ref, o_ref, ob_ref, ss_ref, so_ref = rest
        _cast_block(sw_ref, sg_ref, so_ref)
    else:
        o_ref, ob_ref, ss_ref = rest
    j = pl.program_id(1)
    if nk == 1:
        _emit_stream(h_ref[...] + _dot(x_ref[...], w_ref[...]), o_ref, ob_ref, ss_ref, j)
        return
    k = pl.program_id(2)

    @pl.when(k == 0)
    def _():
        o_ref[...] = h_ref[...] + _dot(x_ref[...], w_ref[...])

    @pl.when(jnp.logical_and(k > 0, k < nk - 1))
    def _():
        o_ref[...] += _dot(x_ref[...], w_ref[...])

    @pl.when(k == nk - 1)
    def _():
        _emit_stream(o_ref[...] + _dot(x_ref[...], w_ref[...]), o_ref, ob_ref, ss_ref, j)


def _ring_mm_kernel(x_hbm, w_hbm, h_ref, *rest, nj, nk, tm, tk, tn, side):
    if side:
        sw_ref, sg_ref, o_ref, ob_ref, ss_ref, so_ref, xbuf, wbuf, sem = rest
        _cast_block(sw_ref, sg_ref, so_ref)
    else:
        o_ref, ob_ref, ss_ref, xbuf, wbuf, sem = rest
    j = pl.program_id(1)
    k = pl.program_id(2)
    total = pl.num_programs(0) * nj * nk
    step = (pl.program_id(0) * nj + j) * nk + k

    def copies(s):
        slot = lax.rem(s, RING_SLOTS)
        kk = lax.rem(s, nk)
        jj = lax.rem(lax.div(s, nk), nj)
        ii = lax.div(s, nk * nj)
        rows = pl.ds(pl.multiple_of(ii * tm, tm), tm)
        mids = pl.ds(pl.multiple_of(kk * tk, tk), tk)
        cols = pl.ds(pl.multiple_of(jj * tn, tn), tn)
        return (pltpu.make_async_copy(x_hbm.at[rows, mids], xbuf.at[slot], sem.at[0, slot]),
                pltpu.make_async_copy(w_hbm.at[mids, cols], wbuf.at[slot], sem.at[1, slot]))

    def start(s):
        for c in copies(s):
            c.start()

    @pl.when(step == 0)
    def _():
        start(step)

    @pl.when(jnp.logical_and(step == 0, total > 1))
    def _():
        start(step + 1)

    @pl.when(step + 2 < total)
    def _():
        start(step + 2)

    for c in copies(step):
        c.wait()
    slot = lax.rem(step, RING_SLOTS)

    @pl.when(k == 0)
    def _():
        o_ref[...] = h_ref[...] + _dot(xbuf[slot], wbuf[slot])

    @pl.when(jnp.logical_and(k > 0, k < nk - 1))
    def _():
        o_ref[...] += _dot(xbuf[slot], wbuf[slot])

    @pl.when(k == nk - 1)
    def _():
        _emit_stream(o_ref[...] + _dot(xbuf[slot], wbuf[slot]), o_ref, ob_ref, ss_ref, j)


def _resid_matmul(x, w, layer, h, *, name, side=None):
    t, k = x.shape
    n = w.shape[-1]
    tk = k if k <= 4096 else _tile(k, 2048)
    nk = k // tk
    assert nk == 1 or nk >= 2
    tm, tn = _tile(t, 1024), _tile(n, 512 if nk == 1 else 1024)
    nj = n // tn
    ring = nk > 1 and w.ndim == 2
    if ring:
        in_specs = [pl.BlockSpec(memory_space=pl.ANY), pl.BlockSpec(memory_space=pl.ANY)]
        scratch = [pltpu.VMEM((RING_SLOTS, tm, tk), x.dtype), pltpu.VMEM((RING_SLOTS, tk, tn), w.dtype),
                   pltpu.SemaphoreType.DMA((2, RING_SLOTS))]
        body = functools.partial(_ring_mm_kernel, nj=nj, nk=nk, tm=tm, tk=tk, tn=tn, side=side is not None)
    else:
        in_specs = [pl.BlockSpec((tm, tk), lambda i, j, kk: (i, kk)), _wspec(w, layer, tk, tn)]
        scratch = []
        body = functools.partial(_resid_mm_kernel, nk=nk, side=side is not None)
    in_specs.append(pl.BlockSpec((tm, tn), lambda i, j, kk: (i, j)))
    args = [x, w, h]
    out_specs, out_shape = _stream_out(t, n, tm, tn)
    if side is not None:
        s_in, s_out, s_shape = _side_cast_specs(side, (t // tm) * nj * nk,
                                                lambda i, j, kk: (i * nj + j) * nk + kk)
        in_specs += s_in
        args += [side.w, side.gain.reshape(*side.gain.shape, 1)]
        out_specs.append(s_out)
        out_shape.append(s_shape)
    return pl.pallas_call(
        body,
        grid=(t // tm, nj, nk),
        in_specs=in_specs,
        out_specs=out_specs,
        out_shape=out_shape,
        scratch_shapes=scratch,
        compiler_params=_params("arbitrary", "arbitrary", "arbitrary"),
        name=name,
    )(*args)


def _ple_kernel(hb_ref, wg_ref, ss_ref, p_ref, wp_ref, h_ref, o_ref, ob_ref, sso_ref, *, d):
    gate = jax.nn.sigmoid(_dot(hb_ref[...], wg_ref[...]) * _rstd(ss_ref, d))
    emb = _dot(p_ref[...].astype(BF16), wp_ref[...])
    _emit_stream(h_ref[...] + emb * gate, o_ref, ob_ref, sso_ref, pl.program_id(1))


def _ple(hb, ss, wg, p, wp, layer, h):
    t, k = hb.shape
    kp = p.shape[1]
    n = wg.shape[-1]
    tm, tn = _tile(t, 1024), _tile(n, 512)
    out_specs, out_shape = _stream_out(t, n, tm, tn)
    return pl.pallas_call(
        functools.partial(_ple_kernel, d=k),
        grid=(t // tm, n // tn),
        in_specs=[pl.BlockSpec((tm, k), lambda i, j: (i, 0)),
                  _wspec(wg, layer, k, tn),
                  pl.BlockSpec((tm, 1), lambda i, j: (i, 0)),
                  pl.BlockSpec((tm, kp), lambda i, j: (i, 0)),
                  _wspec(wp, layer, kp, tn),
                  pl.BlockSpec((tm, tn), lambda i, j: (i, j))],
        out_specs=out_specs,
        out_shape=out_shape,
        compiler_params=_params("parallel", "arbitrary"),
        name="ple",
    )(hb, wg, ss, p, wp, h)


def _attn_kernel(q_ref, k_ref, v_ref, tri_ref, *rest, tq, nheads, nsides):
    side_in, (o_ref, *side_out), (acc_ref, carry_ref) = rest[:2 * nsides], rest[2 * nsides:3 * nsides + 1], rest[3 * nsides + 1:]
    for n in range(nsides):
        _cast_block(side_in[2 * n], side_in[2 * n + 1], side_out[n])
    qi = pl.program_id(2)
    tri = tri_ref[...]
    row = lax.broadcasted_iota(jnp.int32, (tq, tq), 0)
    col = lax.broadcasted_iota(jnp.int32, (tq, tq), 1)
    causal = col < row

    def key_block(kb, masked):
        start = pl.multiple_of(kb * tq, tq)
        heads = range(nheads)
        lanes = [slice(hd * HEAD_DIM, (hd + 1) * HEAD_DIM) for hd in heads]
        zs = [lax.dot_general(q_ref[:, lanes[hd]], k_ref[pl.ds(start, tq), lanes[hd]],
                              (((1,), (1,)), ((), ())), preferred_element_type=F32) for hd in heads]
        tails = []
        for z in zs:
            neg_abs = lax.bitcast_convert_type(
                lax.bitcast_convert_type(z, jnp.uint32) | jnp.uint32(0x80000000), F32)
            softplus = jnp.maximum(z, 0.0) + jnp.log2(1.0 + jnp.exp2(neg_abs))
            if masked:
                softplus = jnp.where(causal, softplus, 0.0)
            hi = softplus.astype(BF16)
            lo = (softplus - hi.astype(F32)).astype(BF16)
            tails.append(_dot(jnp.concatenate([hi, lo], axis=1), tri))
        for hd in heads:
            w = jnp.exp2(zs[hd] + tails[hd] + carry_ref[hd])
            if masked:
                w = jnp.where(causal, w, 0.0)
            acc_ref[:, lanes[hd]] += _dot(w.astype(BF16), v_ref[pl.ds(start, tq), lanes[hd]])
            carry_ref[hd] += tails[hd][:, 0:1]

    acc_ref[...] = jnp.zeros_like(acc_ref)
    carry_ref[...] = jnp.zeros_like(carry_ref)
    key_block(qi, True)

    def body(i, c):
        key_block(qi - 1 - i, False)
        return c

    lax.fori_loop(0, qi, body, 0)
    o_ref[...] = acc_ref[...].astype(o_ref.dtype)


def _attention(qkv, sides, *, batch, seq, heads):
    tq = _tile(seq, 256)
    nq = seq // tq
    nheads = _tile(heads, ATTN_HEADS_PER_STEP)
    hgroups = heads // nheads
    width = nheads * HEAD_DIM
    r = jnp.arange(tq)
    tri = -(r[:, None] >= r[None, :]).astype(BF16)
    tri2 = jnp.concatenate([tri, tri], axis=0)
    nsteps = batch * hgroups * nq
    s_in, s_args, s_out, s_shape = [], [], [], []
    for side in sides:
        specs, out_spec, out_shape = _side_cast_specs(side, nsteps, lambda b, h, i: (b * hgroups + h) * nq + i)
        s_in += specs
        s_args += [side.w, side.gain.reshape(*side.gain.shape, 1)]
        s_out.append(out_spec)
        s_shape.append(out_shape)
    out, *cast = pl.pallas_call(
        functools.partial(_attn_kernel, tq=tq, nheads=nheads, nsides=len(sides)),
        grid=(batch, hgroups, nq),
        in_specs=[pl.BlockSpec((tq, width), lambda b, h, i: (b * nq + i, h)),
                  pl.BlockSpec((seq, width), lambda b, h, i: (b, hgroups + h)),
                  pl.BlockSpec((seq, width), lambda b, h, i: (b, 2 * hgroups + h)),
                  pl.BlockSpec((2 * tq, tq), lambda b, h, i: (0, 0))] + s_in,
        out_specs=[pl.BlockSpec((tq, width), lambda b, h, i: (b * nq + i, h))] + s_out,
        out_shape=[jax.ShapeDtypeStruct((batch * seq, heads * HEAD_DIM), BF16)] + s_shape,
        scratch_shapes=[pltpu.VMEM((tq, width), F32), pltpu.VMEM((nheads, tq, 1), F32)],
        compiler_params=_params("arbitrary", "arbitrary", "arbitrary"),
        name="stickbreak_attn",
    )(qkv, qkv, qkv, tri2, *s_args)
    return out, cast


def _ssm_kernel(u_ref, b_ref, m_ref, c_ref, d_ref, o_ref, x_ref, xb_ref, *, seq):
    ns = SLAB_STATES
    nchunks = seq // SSM_CHUNK
    steps = SSM_CHUNK // SUBLANES

    def rows(k):
        return slice(k * SSM_CHUNK, (k + 1) * SSM_CHUNK)

    def group(i):
        return slice(i * SUBLANES, (i + 1) * SUBLANES)

    def project(k):
        x_ref[rows(k), :] = _dot(u_ref[rows(k), :], b_ref[0])

    def scan(k, v):
        v_re, v_im = v
        for s in range(steps):
            r = group(k * steps + s)
            l_re = m_ref[0, 0]
            l_im = m_ref[0, 1]
            v_re, v_im = (l_re * v_re - l_im * v_im + x_ref[r, 0:ns],
                          l_re * v_im + l_im * v_re + x_ref[r, ns:2 * ns])
            x_ref[r, 0:ns] = v_re
            x_ref[r, ns:2 * ns] = v_im
        return v_re, v_im

    def carry_in(e):
        e_re, e_im = e
        for n, shift in enumerate((1, 2, 4)):
            a_re = m_ref[0, 2 + 2 * n]
            a_im = m_ref[0, 3 + 2 * n]
            s_re = pltpu.roll(e_re, shift, 0)
            s_im = pltpu.roll(e_im, shift, 0)
            e_re, e_im = (e_re + (a_re * s_re - a_im * s_im),
                          e_im + (a_re * s_im + a_im * s_re))
        not_first = m_ref[0, 8]
        return not_first * pltpu.roll(e_re, 1, 0), not_first * pltpu.roll(e_im, 1, 0)

    def fix(k, w):
        w_re, w_im = w
        for s in range(0, steps, 2):
            parts = []
            for i in (k * steps + s, k * steps + s + 1):
                l_re = m_ref[0, 0]
                l_im = m_ref[0, 1]
                w_re, w_im = l_re * w_re - l_im * w_im, l_re * w_im + l_im * w_re
                parts.append(jnp.concatenate([x_ref[group(i), 0:ns] + w_re,
                                              x_ref[group(i), ns:2 * ns] + w_im], axis=1))
            i0 = k * steps + s
            xb_ref[i0 * SUBLANES:(i0 + 2) * SUBLANES, :] = jnp.concatenate(parts, axis=0).astype(BF16)
        return w_re, w_im

    def readout(k):
        y = _dot(xb_ref[rows(k), :], c_ref[0]) + d_ref[...] * u_ref[rows(k), :].astype(F32)
        o_ref[rows(k), :] = jax.nn.gelu(y, approximate=True).astype(o_ref.dtype)

    project(0)
    if nchunks > 1:
        project(1)
    zero = jnp.zeros((SUBLANES, ns), F32)
    v = scan(0, (zero, zero))
    for k in range(1, nchunks):
        if k + 1 < nchunks:
            project(k + 1)
        v = scan(k, v)
    w = fix(0, carry_in(v))
    for k in range(1, nchunks):
        w = fix(k, w)
        readout(k - 1)
    readout(nchunks - 1)


def _ssm_operands(lam_re, lam_im, log_dt, b_re, b_im, c_re, c_im, sub_len):
    g = lam_re.shape[0]
    nslab = g // SLAB_GROUPS
    dt = jnp.exp(log_dt)[:, None]
    lam = lax.complex(lam_re, lam_im)
    lam_bar = jnp.exp(lam * dt)
    b_bar = ((lam_bar - 1.0) / lam)[..., None] * lax.complex(b_re, b_im)

    def block_diag(rows, per_row_group, width, per_col_group):
        n = rows.shape[-1]
        q = jnp.arange(width)
        tiled = jnp.dot(rows, (q[None, :] % n == jnp.arange(n)[:, None]).astype(F32))
        same = (jnp.arange(rows.shape[1]) // per_row_group)[:, None] == (q // per_col_group)[None, :]
        return jnp.where(same, tiled, 0.0)

    def b_block(part):
        t = part.reshape(nslab, SLAB_GROUPS, SSM_STATE, SSM_GROUP).transpose(0, 1, 3, 2)
        return block_diag(t.reshape(nslab, SLAB, SSM_STATE), SSM_GROUP, SLAB_STATES, SSM_STATE)

    def c_block(part):
        t = part.reshape(nslab, SLAB_GROUPS, SSM_GROUP, SSM_STATE).transpose(0, 1, 3, 2)
        return block_diag(t.reshape(nslab, SLAB_STATES, SSM_GROUP), SSM_STATE, SLAB, SSM_GROUP)

    b_mat = jnp.concatenate([b_block(b_bar.real), b_block(b_bar.imag)], axis=2).astype(BF16)
    c_mat = jnp.concatenate([c_block(c_re), c_block(-c_im)], axis=1).astype(BF16)

    lam1 = lam_bar.reshape(nslab, 1, SLAB_STATES)
    big1, sq, n = jnp.ones_like(lam1), lam1, sub_len
    while n:
        if n & 1:
            big1 = big1 * sq
        sq, n = sq * sq, n >> 1
    big2 = big1 * big1
    big4 = big2 * big2
    rows = jnp.arange(SUBLANES)[None, :, None]
    every = jnp.broadcast_to(lam1, (nslab, SUBLANES, SLAB_STATES))
    planes = [every.real, every.imag]
    for big, shift in ((big1, 1), (big2, 2), (big4, 4)):
        a = jnp.where(rows >= shift, big, 0.0)
        planes += [a.real, a.imag]
    planes.append(jnp.broadcast_to((rows >= 1).astype(F32), (nslab, SUBLANES, SLAB_STATES)))
    mult = jnp.stack(planes, axis=1).astype(F32)
    return b_mat, mult, c_mat


def _ssm(u_perm, b_mat, mult, c_mat, d_skip, layer, *, batch, seq):
    nslab = b_mat.shape[1]
    width = nslab * SLAB
    sub_len = seq // SUBLANES
    assert seq % SSM_CHUNK == 0 and (SSM_CHUNK // SUBLANES) % 2 == 0
    return pl.pallas_call(
        functools.partial(_ssm_kernel, seq=seq),
        grid=(batch, nslab),
        in_specs=[pl.BlockSpec((seq, SLAB), lambda b, s: (b, s)),
                  pl.BlockSpec((None, 1, SLAB, 2 * SLAB_STATES), lambda b, s: (layer, s, 0, 0)),
                  pl.BlockSpec((None, 1, 9, SUBLANES, SLAB_STATES), lambda b, s: (layer, s, 0, 0, 0)),
                  pl.BlockSpec((None, 1, 2 * SLAB_STATES, SLAB), lambda b, s: (layer, s, 0, 0)),
                  pl.BlockSpec((None, 1, SLAB), lambda b, s: (layer, 0, s))],
        out_specs=pl.BlockSpec((seq, SLAB), lambda b, s: (b, s)),
        out_shape=jax.ShapeDtypeStruct((batch * seq, width), BF16),
        scratch_shapes=[pltpu.VMEM((seq, 2 * SLAB_STATES), F32), pltpu.VMEM((seq, 2 * SLAB_STATES), BF16)],
        compiler_params=_params("parallel", "parallel"),
        name="s5_ssm",
    )(u_perm, b_mat, mult, c_mat, d_skip.reshape(-1, 1, width))


def kernel(x, p, g_mix, w_in, w_br_attn, lam_re, lam_im, log_dt, b_re, b_im, c_re, c_im, d_skip, w_glu, w_br_ssm, w_o, g_mlp, w_ff1, w_ff2, g_ple, w_ple_gate, w_ple, g_final):
    bsz, seq, d = x.shape
    depth = w_in.shape[0]
    attn_w = w_br_attn.shape[1]
    ssm_w = w_br_ssm.shape[1]
    heads = attn_w // HEAD_DIM
    t = bsz * seq
    assert w_in.shape[2] == 3 * attn_w + ssm_w + 2 * d

    wb_in = _to_bf16(w_in, g_mix, layer=0)
    wb_ple = _to_bf16(w_ple)
    b_mat, mult, c_mat = jax.vmap(functools.partial(_ssm_operands, sub_len=seq // SUBLANES))(
        lam_re, lam_im, log_dt, b_re, b_im, c_re, c_im)

    def plain(w):
        return jnp.ones(w.shape[:2], F32)

    h = x.reshape(t, d)
    hb, ss = _enter(h)
    for i in range(depth):
        qkv, gates = _proj(hb, ss, wb_in, i, attn_w=attn_w, ssm_w=ssm_w, d_model=d)
        u_perm = _proj_u(hb, ss, wb_in, i, col0=3 * attn_w, width=ssm_w, batch=bsz, seq=seq)
        attn, (wb_ff1, wb_o, wb_ple_gate, wb_br_attn, wb_br_ssm, wb_glu) = _attention(
            qkv,
            [_SideCast(w_ff1, g_mlp, i), _SideCast(w_o, plain(w_o), i), _SideCast(w_ple_gate, g_ple, i),
             _SideCast(w_br_attn, plain(w_br_attn), i), _SideCast(w_br_ssm, plain(w_br_ssm), i),
             _SideCast(w_glu, plain(w_glu), i)],
            batch=bsz, seq=seq, heads=heads)
        y_perm = _ssm(u_perm, b_mat, mult, c_mat, d_skip, i, batch=bsz, seq=seq)
        ssm3 = _glu(y_perm, wb_glu, i, batch=bsz, seq=seq)
        merged = _merge(attn, wb_br_attn, ssm3, wb_br_ssm, gates, i, seq=seq)
        h, hb, ss = _resid_matmul(merged, wb_o, i, h, name="out_proj")

        act, wb_ff2 = _normed_relu2_matmul(hb, ss, wb_ff1, i, name="ff1",
                                           side=_SideCast(w_ff2, plain(w_ff2), i))
        if i + 1 < depth:
            h, hb, ss, wb_in = _resid_matmul(act, wb_ff2, i, h, name="ff2",
                                             side=_SideCast(w_in, g_mix, i + 1))
        else:
            h, hb, ss = _resid_matmul(act, wb_ff2, i, h, name="ff2")

        h, hb, ss = _ple(hb, ss, wb_ple_gate, p[i].reshape(t, -1), wb_ple, i, h)
    return _rmsnorm(h, g_final, x.dtype).reshape(bsz, seq, d)
```

```python
import functools
import math
from typing import NamedTuple

import jax
import jax.numpy as jnp
from jax import lax
from jax.experimental import pallas as pl
from jax.experimental.pallas import tpu as pltpu

F32 = jnp.float32
BF16 = jnp.bfloat16

RMS_EPS = 1e-6
HEAD_DIM = 128
ATTN_Q_SCALE = HEAD_DIM ** -0.5 * math.log2(math.e)
ATTN_HEADS_PER_STEP = 8
SSM_GROUP = 16
SSM_STATE = 64
SLAB = 256
SLAB_GROUPS = SLAB // SSM_GROUP
SLAB_STATES = SLAB_GROUPS * SSM_STATE
SUBLANES = 8
LANES = 128
SSM_CHUNK = 512
RING_SLOTS = 3
V7X_VMEM_LIMIT = 56 * 1024 * 1024
CAST_BLOCK_BYTES = 8 * 1024 * 1024


def _tile(dim, pref):
    if dim % pref == 0:
        return pref
    assert dim < pref, (dim, pref)
    return dim


def _params(*sem):
    return pltpu.CompilerParams(dimension_semantics=sem, vmem_limit_bytes=V7X_VMEM_LIMIT)


def _dot(a, b):
    return jnp.dot(a, b, preferred_element_type=F32)


def _wspec(w, layer, k, tn, joff=0):
    if w.ndim == 2:
        return pl.BlockSpec((k, tn), lambda i, j, *kk: (kk[0] if kk else 0, j + joff))
    return pl.BlockSpec((None, k, tn), lambda i, j, *kk: (layer, kk[0] if kk else 0, j + joff))


class _SideCast(NamedTuple):
    w: jax.Array
    gain: jax.Array
    layer: int


def _side_cast_specs(side, nsteps, step_of):
    depth, k, n = side.w.shape
    nblk = max(b for b in range(1, nsteps + 1) if k % b == 0 and (k // b) % 16 == 0)
    bk = k // nblk

    def blk(*g):
        return jnp.minimum(step_of(*g), nblk - 1)

    in_specs = [pl.BlockSpec((None, bk, n), lambda *g: (side.layer, blk(*g), 0)),
                pl.BlockSpec((None, bk, 1), lambda *g: (side.layer, blk(*g), 0))]
    return in_specs, pl.BlockSpec((bk, n), lambda *g: (blk(*g), 0)), jax.ShapeDtypeStruct((k, n), BF16)


def _cast_block(w_ref, g_ref, o_ref):
    o_ref[...] = (w_ref[...] * g_ref[...]).astype(o_ref.dtype)


def _rstd(ss_ref, d):
    return lax.rsqrt(ss_ref[...] * (1.0 / d) + RMS_EPS)


def _emit_stream(h, h_ref, hb_ref, ss_ref, j):
    h_ref[...] = h
    hb_ref[...] = h.astype(BF16)
    row_ss = jnp.sum(h * h, axis=1, keepdims=True)

    @pl.when(j == 0)
    def _():
        ss_ref[...] = row_ss

    @pl.when(j > 0)
    def _():
        ss_ref[...] += row_ss


def _stream_out(t, n, tm, tn):
    specs = [pl.BlockSpec((tm, tn), lambda i, j, *kk: (i, j)),
             pl.BlockSpec((tm, tn), lambda i, j, *kk: (i, j)),
             pl.BlockSpec((tm, 1), lambda i, j, *kk: (i, 0))]
    shapes = [jax.ShapeDtypeStruct((t, n), F32), jax.ShapeDtypeStruct((t, n), BF16),
              jax.ShapeDtypeStruct((t, 1), F32)]
    return specs, shapes


def _to_bf16(w, gain=None, layer=None):
    depth, k, n = w.shape
    bk = max(16, min(k, CAST_BLOCK_BYTES // (4 * n)))
    assert k % bk == 0, (k, bk)
    if gain is None:
        gain = jnp.ones((depth, k), F32)
    if layer is None:
        grid, first = (depth, k // bk), 0
        out_spec = pl.BlockSpec((None, bk, n), lambda a, r: (a, r, 0))
        out_shape = jax.ShapeDtypeStruct(w.shape, BF16)
    else:
        grid, first = (1, k // bk), layer
        out_spec = pl.BlockSpec((bk, n), lambda a, r: (r, 0))
        out_shape = jax.ShapeDtypeStruct((k, n), BF16)
    return pl.pallas_call(
        _cast_block,
        grid=grid,
        in_specs=[pl.BlockSpec((None, bk, n), lambda a, r: (a + first, r, 0)),
                  pl.BlockSpec((None, bk, 1), lambda a, r: (a + first, r, 0))],
        out_specs=out_spec,
        out_shape=out_shape,
        compiler_params=_params("parallel", "parallel"),
        name="cast_bf16",
    )(w, gain.reshape(depth, k, 1))


def _enter_kernel(x_ref, xb_ref, ss_ref):
    x = x_ref[...]
    xb_ref[...] = x.astype(BF16)
    ss_ref[...] = jnp.sum(x * x, axis=1, keepdims=True)


def _enter(x):
    t, d = x.shape
    tm = _tile(t, 256)
    return pl.pallas_call(
        _enter_kernel,
        grid=(t // tm,),
        in_specs=[pl.BlockSpec((tm, d), lambda i: (i, 0))],
        out_specs=[pl.BlockSpec((tm, d), lambda i: (i, 0)), pl.BlockSpec((tm, 1), lambda i: (i, 0))],
        out_shape=[jax.ShapeDtypeStruct((t, d), BF16), jax.ShapeDtypeStruct((t, 1), F32)],
        compiler_params=_params("parallel"),
        name="enter",
    )(x)


def _rmsnorm_kernel(x_ref, g_ref, o_ref):
    x = x_ref[...]
    ms = jnp.mean(x * x, axis=-1, keepdims=True)
    o_ref[...] = (x * lax.rsqrt(ms + RMS_EPS) * g_ref[...]).astype(o_ref.dtype)


def _rmsnorm(x, g, out_dtype):
    t, d = x.shape
    tm = _tile(t, 256)
    return pl.pallas_call(
        _rmsnorm_kernel,
        grid=(t // tm,),
        in_specs=[pl.BlockSpec((tm, d), lambda i: (i, 0)),
                  pl.BlockSpec((1, d), lambda i: (0, 0))],
        out_specs=pl.BlockSpec((tm, d), lambda i: (i, 0)),
        out_shape=jax.ShapeDtypeStruct((t, d), out_dtype),
        compiler_params=_params("parallel"),
        name="rmsnorm",
    )(x, g.reshape(1, d))


def _normed_mm_kernel(hb_ref, w_ref, ss_ref, *rest, d, side):
    if side:
        sw_ref, sg_ref, o_ref, so_ref = rest
        _cast_block(sw_ref, sg_ref, so_ref)
    else:
        (o_ref,) = rest
    acc = _dot(hb_ref[...], w_ref[...]) * _rstd(ss_ref, d)
    o_ref[...] = jnp.square(jnp.maximum(acc, 0.0)).astype(o_ref.dtype)


def _normed_relu2_matmul(hb, ss, w, layer, *, name, side=None):
    t, k = hb.shape
    ncols = w.shape[-1]
    tm, tn = _tile(t, 1024), _tile(ncols, 1024)
    nj = ncols // tn
    in_specs = [pl.BlockSpec((tm, k), lambda i, j: (i, 0)),
                _wspec(w, layer, k, tn),
                pl.BlockSpec((tm, 1), lambda i, j: (i, 0))]
    args = [hb, w, ss]
    out_specs = [pl.BlockSpec((tm, tn), lambda i, j: (i, j))]
    out_shape = [jax.ShapeDtypeStruct((t, ncols), BF16)]
    if side is not None:
        s_in, s_out, s_shape = _side_cast_specs(side, (t // tm) * nj, lambda i, j: i * nj + j)
        in_specs += s_in
        args += [side.w, side.gain.reshape(*side.gain.shape, 1)]
        out_specs.append(s_out)
        out_shape.append(s_shape)
    outs = pl.pallas_call(
        functools.partial(_normed_mm_kernel, d=k, side=side is not None),
        grid=(t // tm, nj),
        in_specs=in_specs,
        out_specs=out_specs,
        out_shape=out_shape,
        compiler_params=_params("parallel", "parallel"),
        name=name,
    )(*args)
    return outs if side is not None else outs[0]


def _proj_kernel(hb_ref, w_ref, ss_ref, qkv_ref, gates_ref, *, d, nq, nqkv):
    j = pl.program_id(1)

    @pl.when(j < nqkv)
    def _():
        scale = jnp.where(j < nq, ATTN_Q_SCALE, 1.0)
        qkv_ref[...] = (_dot(hb_ref[...], w_ref[...]) * (_rstd(ss_ref, d) * scale)).astype(qkv_ref.dtype)

    @pl.when(j >= nqkv)
    def _():
        gates_ref[...] = jax.nn.sigmoid(_dot(hb_ref[...], w_ref[...]) * _rstd(ss_ref, d)).astype(gates_ref.dtype)


def _proj(hb, ss, w, layer, *, attn_w, ssm_w, d_model):
    t, k = hb.shape
    tm = _tile(t, 1024)
    tn = next(c for c in (1024, 512, 256, 128)
              if attn_w % c == 0 and ssm_w % c == 0 and (2 * d_model) % c == 0)
    nq, nqkv, nu, ng = attn_w // tn, 3 * attn_w // tn, ssm_w // tn, 2 * d_model // tn

    def wcol(j):
        return jnp.where(j < nqkv, j, j + nu)

    if w.ndim == 2:
        w_spec = pl.BlockSpec((k, tn), lambda i, j: (0, wcol(j)))
    else:
        w_spec = pl.BlockSpec((None, k, tn), lambda i, j: (layer, 0, wcol(j)))
    return pl.pallas_call(
        functools.partial(_proj_kernel, d=k, nq=nq, nqkv=nqkv),
        grid=(t // tm, nqkv + ng),
        in_specs=[pl.BlockSpec((tm, k), lambda i, j: (i, 0)),
                  w_spec,
                  pl.BlockSpec((tm, 1), lambda i, j: (i, 0))],
        out_specs=[pl.BlockSpec((tm, tn), lambda i, j: (i, jnp.minimum(j, nqkv - 1))),
                   pl.BlockSpec((tm, tn), lambda i, j: (i, jnp.maximum(j - nqkv, 0)))],
        out_shape=[jax.ShapeDtypeStruct((t, 3 * attn_w), BF16),
                   jax.ShapeDtypeStruct((t, 2 * d_model), BF16)],
        compiler_params=_params("parallel", "arbitrary"),
        name="proj",
    )(hb, w, ss)


def _proj_u_kernel(hb_ref, w_ref, ss_ref, o_ref, stage_ref, *, d, sub_len):
    acc = _dot(hb_ref[...], w_ref[...]) * _rstd(ss_ref, d)
    for c in range(acc.shape[1] // LANES):
        for r in range(SUBLANES):
            stage_ref[c, pl.ds(r, sub_len, stride=SUBLANES), :] = (
                acc[r * sub_len:(r + 1) * sub_len, c * LANES:(c + 1) * LANES])
    for c in range(acc.shape[1] // LANES):
        o_ref[:, c * LANES:(c + 1) * LANES] = stage_ref[c].astype(o_ref.dtype)


def _proj_u(hb, ss, w, layer, *, col0, width, batch, seq):
    t, k = hb.shape
    sub_len = seq // SUBLANES
    tn = _tile(width, 256)
    assert col0 % tn == 0
    joff = col0 // tn
    if w.ndim == 2:
        w_spec = pl.BlockSpec((k, tn), lambda b, j: (0, j + joff))
    else:
        w_spec = pl.BlockSpec((None, k, tn), lambda b, j: (layer, 0, j + joff))
    return pl.pallas_call(
        functools.partial(_proj_u_kernel, d=k, sub_len=sub_len),
        grid=(batch, width // tn),
        in_specs=[pl.BlockSpec((seq, k), lambda b, j: (b, 0)),
                  w_spec,
                  pl.BlockSpec((seq, 1), lambda b, j: (b, 0))],
        out_specs=pl.BlockSpec((seq, tn), lambda b, j: (b, j)),
        out_shape=jax.ShapeDtypeStruct((t, width), BF16),
        scratch_shapes=[pltpu.VMEM((tn // LANES, seq, LANES), F32)],
        compiler_params=_params("parallel", "parallel"),
        name="proj_u",
    )(hb, w, ss)


def _glu_kernel(y_ref, w_ref, o_ref, stage_ref, *, width):
    y = y_ref[...]
    res = y.astype(F32) * jax.nn.sigmoid(_dot(y, w_ref[...]))
    ntile = width // LANES
    sub_rows = res.shape[0] // SUBLANES
    for c in range(ntile):
        stage_ref[c] = res[:, c * LANES:(c + 1) * LANES]
    for r in range(SUBLANES):
        for c in range(ntile):
            lane0 = r * width + c * LANES
            o_ref[:, lane0:lane0 + LANES] = (
                stage_ref[c, pl.ds(r, sub_rows, stride=SUBLANES), :].astype(o_ref.dtype))


def _glu(y_perm, w, layer, *, batch, seq):
    t, width = y_perm.shape
    assert w.shape[-1] == width and w.shape[-2] == width
    tm = _tile(seq, 512)
    tiles = seq // tm
    return pl.pallas_call(
        functools.partial(_glu_kernel, width=width),
        grid=(t // tm,),
        in_specs=[pl.BlockSpec((tm, width), lambda i: (i, 0)),
                  (pl.BlockSpec((width, width), lambda i: (0, 0)) if w.ndim == 2
                   else pl.BlockSpec((None, width, width), lambda i: (layer, 0, 0)))],
        out_specs=pl.BlockSpec((None, tm // SUBLANES, SUBLANES * width), lambda i: (i // tiles, i % tiles, 0)),
        out_shape=jax.ShapeDtypeStruct((batch, seq // SUBLANES, SUBLANES * width), BF16),
        scratch_shapes=[pltpu.VMEM((width // LANES, tm, LANES), F32)],
        compiler_params=_params("parallel"),
        name="glu",
    )(y_perm, w)


def _merge_kernel(a_ref, wa_ref, s_ref, ws_ref, ga_ref, gs_ref, o_ref, *, nr, sub_len, width):
    up_a = _dot(a_ref[...], wa_ref[...])
    for rl in range(nr):
        rows = slice(rl * sub_len, (rl + 1) * sub_len)
        up_s = _dot(s_ref[:, rl * width:(rl + 1) * width], ws_ref[...])
        o_ref[rows, :] = (ga_ref[rows, :].astype(F32) * up_a[rows, :]
                          + gs_ref[rows, :].astype(F32) * up_s).astype(o_ref.dtype)


def _merge(attn, wa, ssm3, ws, gates, layer, *, seq):
    t, ka = attn.shape
    sub_len = seq // SUBLANES
    assert ssm3.shape[1] == sub_len
    ks = ssm3.shape[2] // SUBLANES
    d = wa.shape[-1]
    tm, tn = _tile(seq, 1024), _tile(d, 1024)
    assert tm % sub_len == 0
    nr = tm // sub_len
    tiles = seq // tm
    nj = d // tn
    return pl.pallas_call(
        functools.partial(_merge_kernel, nr=nr, sub_len=sub_len, width=ks),
        grid=(t // tm, nj),
        in_specs=[pl.BlockSpec((tm, ka), lambda i, j: (i, 0)),
                  _wspec(wa, layer, ka, tn),
                  pl.BlockSpec((None, sub_len, nr * ks), lambda i, j: (i // tiles, 0, i % tiles)),
                  _wspec(ws, layer, ks, tn),
                  pl.BlockSpec((tm, tn), lambda i, j: (i, j)),
                  pl.BlockSpec((tm, tn), lambda i, j: (i, j + nj))],
        out_specs=pl.BlockSpec((tm, tn), lambda i, j: (i, j)),
        out_shape=jax.ShapeDtypeStruct((t, d), BF16),
        compiler_params=_params("parallel", "parallel"),
        name="merge",
    )(attn, wa, ssm3, ws, gates, gates)


def _resid_mm_kernel(x_ref, w_ref, h_ref, *rest, nk, side):
    if side:
        sw_ref, sg_ref, o_ref, ob_ref, ss_ref, so_ref = rest
        _cast_block(sw_ref, sg_ref, so_ref)
    else:
        o_ref, ob_ref, ss_ref = rest
    j = pl.program_id(1)
    if nk == 1:
        _emit_stream(h_ref[...] + _dot(x_ref[...], w_ref[...]), o_ref, ob_ref, ss_ref, j)
        return
    k = pl.program_id(2)

    @pl.when(k == 0)
    def _():
        o_ref[...] = h_ref[...] + _dot(x_ref[...], w_ref[...])

    @pl.when(jnp.logical_and(k > 0, k < nk - 1))
    def _():
        o_ref[...] += _dot(x_ref[...], w_ref[...])

    @pl.when(k == nk - 1)
    def _():
        _emit_stream(o_ref[...] + _dot(x_ref[...], w_ref[...]), o_ref, ob_ref, ss_ref, j)


def _ring_mm_kernel(x_in, w_hbm, h_ref, *rest, nj, nk, tm, tk, tn, side):
    ring_x = nk > 1
    if side:
        sw_ref, sg_ref, o_ref, ob_ref, ss_ref, so_ref, *scratch = rest
        _cast_block(sw_ref, sg_ref, so_ref)
    else:
        o_ref, ob_ref, ss_ref, *scratch = rest
    if ring_x:
        xbuf, wbuf, sem = scratch
    else:
        wbuf, sem = scratch
    j = pl.program_id(1)
    k = pl.program_id(2)
    total = pl.num_programs(0) * nj * nk
    step = (pl.program_id(0) * nj + j) * nk + k

    def copies(s):
        slot = lax.rem(s, RING_SLOTS)
        kk = lax.rem(s, nk)
        jj = lax.rem(lax.div(s, nk), nj)
        ii = lax.div(s, nk * nj)
        rows = pl.ds(pl.multiple_of(ii * tm, tm), tm)
        mids = pl.ds(pl.multiple_of(kk * tk, tk), tk)
        cols = pl.ds(pl.multiple_of(jj * tn, tn), tn)
        out = [pltpu.make_async_copy(w_hbm.at[mids, cols], wbuf.at[slot], sem.at[1, slot])]
        if ring_x:
            out.append(pltpu.make_async_copy(x_in.at[rows, mids], xbuf.at[slot], sem.at[0, slot]))
        return out

    def start(s):
        for c in copies(s):
            c.start()

    @pl.when(step == 0)
    def _():
        start(step)

    @pl.when(jnp.logical_and(step == 0, total > 1))
    def _():
        start(step + 1)

    @pl.when(step + 2 < total)
    def _():
        start(step + 2)

    for c in copies(step):
        c.wait()
    slot = lax.rem(step, RING_SLOTS)

    def product():
        return _dot(xbuf[slot] if ring_x else x_in[...], wbuf[slot])

    if nk == 1:
        _emit_stream(h_ref[...] + product(), o_ref, ob_ref, ss_ref, j)
        return

    @pl.when(k == 0)
    def _():
        o_ref[...] = h_ref[...] + product()

    @pl.when(jnp.logical_and(k > 0, k < nk - 1))
    def _():
        o_ref[...] += product()

    @pl.when(k == nk - 1)
    def _():
        _emit_stream(o_ref[...] + product(), o_ref, ob_ref, ss_ref, j)


def _resid_matmul(x, w, layer, h, *, name, side=None):
    t, k = x.shape
    n = w.shape[-1]
    tk = k if k <= 4096 else _tile(k, 2048)
    nk = k // tk
    assert nk == 1 or nk >= 2
    tm, tn = _tile(t, 1024), _tile(n, 512 if nk == 1 else 1024)
    nj = n // tn
    ring = w.ndim == 2
    if ring:
        x_spec = pl.BlockSpec(memory_space=pl.ANY) if nk > 1 else pl.BlockSpec((tm, tk), lambda i, j, kk: (i, kk))
        in_specs = [x_spec, pl.BlockSpec(memory_space=pl.ANY)]
        scratch = ([pltpu.VMEM((RING_SLOTS, tm, tk), x.dtype)] if nk > 1 else []) + [
            pltpu.VMEM((RING_SLOTS, tk, tn), w.dtype), pltpu.SemaphoreType.DMA((2, RING_SLOTS))]
        body = functools.partial(_ring_mm_kernel, nj=nj, nk=nk, tm=tm, tk=tk, tn=tn, side=side is not None)
    else:
        in_specs = [pl.BlockSpec((tm, tk), lambda i, j, kk: (i, kk)), _wspec(w, layer, tk, tn)]
        scratch = []
        body = functools.partial(_resid_mm_kernel, nk=nk, side=side is not None)
    in_specs.append(pl.BlockSpec((tm, tn), lambda i, j, kk: (i, j)))
    args = [x, w, h]
    out_specs, out_shape = _stream_out(t, n, tm, tn)
    if side is not None:
        s_in, s_out, s_shape = _side_cast_specs(side, (t // tm) * nj * nk,
                                                lambda i, j, kk: (i * nj + j) * nk + kk)
        in_specs += s_in
        args += [side.w, side.gain.reshape(*side.gain.shape, 1)]
        out_specs.append(s_out)
        out_shape.append(s_shape)
    return pl.pallas_call(
        body,
        grid=(t // tm, nj, nk),
        in_specs=in_specs,
        out_specs=out_specs,
        out_shape=out_shape,
        scratch_shapes=scratch,
        compiler_params=_params("arbitrary", "arbitrary", "arbitrary"),
        name=name,
    )(*args)


def _ple_kernel(hb_ref, wg_ref, ss_ref, p_ref, wp_ref, h_ref, o_ref, ob_ref, sso_ref, *, d):
    gate = jax.nn.sigmoid(_dot(hb_ref[...], wg_ref[...]) * _rstd(ss_ref, d))
    emb = _dot(p_ref[...].astype(BF16), wp_ref[...])
    _emit_stream(h_ref[...] + emb * gate, o_ref, ob_ref, sso_ref, pl.program_id(1))


def _ple(hb, ss, wg, p, wp, layer, h):
    t, k = hb.shape
    kp = p.shape[1]
    n = wg.shape[-1]
    tm, tn = _tile(t, 1024), _tile(n, 512)
    out_specs, out_shape = _stream_out(t, n, tm, tn)
    return pl.pallas_call(
        functools.partial(_ple_kernel, d=k),
        grid=(t // tm, n // tn),
        in_specs=[pl.BlockSpec((tm, k), lambda i, j: (i, 0)),
                  _wspec(wg, layer, k, tn),
                  pl.BlockSpec((tm, 1), lambda i, j: (i, 0)),
                  pl.BlockSpec((tm, kp), lambda i, j: (i, 0)),
                  _wspec(wp, layer, kp, tn),
                  pl.BlockSpec((tm, tn), lambda i, j: (i, j))],
        out_specs=out_specs,
        out_shape=out_shape,
        compiler_params=_params("parallel", "arbitrary"),
        name="ple",
    )(hb, wg, ss, p, wp, h)


def _attn_kernel(q_ref, k_ref, v_ref, tri_ref, *rest, tq, nheads, nsides):
    side_in, (o_ref, *side_out), (acc_ref, carry_ref) = rest[:2 * nsides], rest[2 * nsides:3 * nsides + 1], rest[3 * nsides + 1:]
    for n in range(nsides):
        _cast_block(side_in[2 * n], side_in[2 * n + 1], side_out[n])
    qi = pl.program_id(2)
    tri = tri_ref[...]
    row = lax.broadcasted_iota(jnp.int32, (tq, tq), 0)
    col = lax.broadcasted_iota(jnp.int32, (tq, tq), 1)
    causal = col < row

    def key_block(kb, masked):
        start = pl.multiple_of(kb * tq, tq)
        heads = range(nheads)
        lanes = [slice(hd * HEAD_DIM, (hd + 1) * HEAD_DIM) for hd in heads]
        zs = [lax.dot_general(q_ref[:, lanes[hd]], k_ref[pl.ds(start, tq), lanes[hd]],
                              (((1,), (1,)), ((), ())), preferred_element_type=F32) for hd in heads]
        tails = []
        for z in zs:
            neg_abs = lax.bitcast_convert_type(
                lax.bitcast_convert_type(z, jnp.uint32) | jnp.uint32(0x80000000), F32)
            softplus = jnp.maximum(z, 0.0) + jnp.log2(1.0 + jnp.exp2(neg_abs))
            if masked:
                softplus = jnp.where(causal, softplus, 0.0)
            hi = softplus.astype(BF16)
            lo = (softplus - hi.astype(F32)).astype(BF16)
            tails.append(_dot(jnp.concatenate([hi, lo], axis=1), tri))
        for hd in heads:
            w = jnp.exp2(zs[hd] + tails[hd] + carry_ref[hd])
            if masked:
                w = jnp.where(causal, w, 0.0)
            acc_ref[:, lanes[hd]] += _dot(w.astype(BF16), v_ref[pl.ds(start, tq), lanes[hd]])
            carry_ref[hd] += tails[hd][:, 0:1]

    acc_ref[...] = jnp.zeros_like(acc_ref)
    carry_ref[...] = jnp.zeros_like(carry_ref)
    key_block(qi, True)

    def body(i, c):
        key_block(qi - 1 - i, False)
        return c

    lax.fori_loop(0, qi, body, 0)
    o_ref[...] = acc_ref[...].astype(o_ref.dtype)


def _attention(qkv, sides, *, batch, seq, heads):
    tq = _tile(seq, 256)
    nq = seq // tq
    nheads = _tile(heads, ATTN_HEADS_PER_STEP)
    hgroups = heads // nheads
    width = nheads * HEAD_DIM
    r = jnp.arange(tq)
    tri = -(r[:, None] >= r[None, :]).astype(BF16)
    tri2 = jnp.concatenate([tri, tri], axis=0)
    nsteps = batch * hgroups * nq
    s_in, s_args, s_out, s_shape = [], [], [], []
    for side in sides:
        specs, out_spec, out_shape = _side_cast_specs(side, nsteps, lambda b, h, i: (b * hgroups + h) * nq + i)
        s_in += specs
        s_args += [side.w, side.gain.reshape(*side.gain.shape, 1)]
        s_out.append(out_spec)
        s_shape.append(out_shape)
    out, *cast = pl.pallas_call(
        functools.partial(_attn_kernel, tq=tq, nheads=nheads, nsides=len(sides)),
        grid=(batch, hgroups, nq),
        in_specs=[pl.BlockSpec((tq, width), lambda b, h, i: (b * nq + i, h)),
                  pl.BlockSpec((seq, width), lambda b, h, i: (b, hgroups + h)),
                  pl.BlockSpec((seq, width), lambda b, h, i: (b, 2 * hgroups + h)),
                  pl.BlockSpec((2 * tq, tq), lambda b, h, i: (0, 0))] + s_in,
        out_specs=[pl.BlockSpec((tq, width), lambda b, h, i: (b * nq + i, h))] + s_out,
        out_shape=[jax.ShapeDtypeStruct((batch * seq, heads * HEAD_DIM), BF16)] + s_shape,
        scratch_shapes=[pltpu.VMEM((tq, width), F32), pltpu.VMEM((nheads, tq, 1), F32)],
        compiler_params=_params("arbitrary", "arbitrary", "arbitrary"),
        name="stickbreak_attn",
    )(qkv, qkv, qkv, tri2, *s_args)
    return out, cast


def _ssm_kernel(u_ref, b_ref, m_ref, c_ref, d_ref, o_ref, x_ref, xb_ref, *, seq):
    ns = SLAB_STATES
    nchunks = seq // SSM_CHUNK
    steps = SSM_CHUNK // SUBLANES

    def rows(k):
        return slice(k * SSM_CHUNK, (k + 1) * SSM_CHUNK)

    def group(i):
        return slice(i * SUBLANES, (i + 1) * SUBLANES)

    def project(k):
        x_ref[rows(k), :] = _dot(u_ref[rows(k), :], b_ref[0])

    def scan(k, v):
        v_re, v_im = v
        for s in range(steps):
            r = group(k * steps + s)
            l_re = m_ref[0, 0]
            l_im = m_ref[0, 1]
            v_re, v_im = (l_re * v_re - l_im * v_im + x_ref[r, 0:ns],
                          l_re * v_im + l_im * v_re + x_ref[r, ns:2 * ns])
            x_ref[r, 0:ns] = v_re
            x_ref[r, ns:2 * ns] = v_im
        return v_re, v_im

    def carry_in(e):
        e_re, e_im = e
        for n, shift in enumerate((1, 2, 4)):
            a_re = m_ref[0, 2 + 2 * n]
            a_im = m_ref[0, 3 + 2 * n]
            s_re = pltpu.roll(e_re, shift, 0)
            s_im = pltpu.roll(e_im, shift, 0)
            e_re, e_im = (e_re + (a_re * s_re - a_im * s_im),
                          e_im + (a_re * s_im + a_im * s_re))
        not_first = m_ref[0, 8]
        return not_first * pltpu.roll(e_re, 1, 0), not_first * pltpu.roll(e_im, 1, 0)

    def fix(k, w):
        w_re, w_im = w
        for s in range(0, steps, 2):
            parts = []
            for i in (k * steps + s, k * steps + s + 1):
                l_re = m_ref[0, 0]
                l_im = m_ref[0, 1]
                w_re, w_im = l_re * w_re - l_im * w_im, l_re * w_im + l_im * w_re
                parts.append(jnp.concatenate([x_ref[group(i), 0:ns] + w_re,
                                              x_ref[group(i), ns:2 * ns] + w_im], axis=1))
            i0 = k * steps + s
            xb_ref[i0 * SUBLANES:(i0 + 2) * SUBLANES, :] = jnp.concatenate(parts, axis=0).astype(BF16)
        return w_re, w_im

    def readout(k):
        y = _dot(xb_ref[rows(k), :], c_ref[0]) + d_ref[...] * u_ref[rows(k), :].astype(F32)
        o_ref[rows(k), :] = jax.nn.gelu(y, approximate=True).astype(o_ref.dtype)

    project(0)
    if nchunks > 1:
        project(1)
    zero = jnp.zeros((SUBLANES, ns), F32)
    v = scan(0, (zero, zero))
    for k in range(1, nchunks):
        if k + 1 < nchunks:
            project(k + 1)
        v = scan(k, v)
    w = fix(0, carry_in(v))
    for k in range(1, nchunks):
        w = fix(k, w)
        readout(k - 1)
    readout(nchunks - 1)


def _ssm_operands(lam_re, lam_im, log_dt, b_re, b_im, c_re, c_im, sub_len):
    g = lam_re.shape[0]
    nslab = g // SLAB_GROUPS
    dt = jnp.exp(log_dt)[:, None]
    lam = lax.complex(lam_re, lam_im)
    lam_bar = jnp.exp(lam * dt)
    b_bar = ((lam_bar - 1.0) / lam)[..., None] * lax.complex(b_re, b_im)

    def block_diag(rows, per_row_group, width, per_col_group):
        n = rows.shape[-1]
        q = jnp.arange(width)
        tiled = jnp.dot(rows, (q[None, :] % n == jnp.arange(n)[:, None]).astype(F32))
        same = (jnp.arange(rows.shape[1]) // per_row_group)[:, None] == (q // per_col_group)[None, :]
        return jnp.where(same, tiled, 0.0)

    def b_block(part):
        t = part.reshape(nslab, SLAB_GROUPS, SSM_STATE, SSM_GROUP).transpose(0, 1, 3, 2)
        return block_diag(t.reshape(nslab, SLAB, SSM_STATE), SSM_GROUP, SLAB_STATES, SSM_STATE)

    def c_block(part):
        t = part.reshape(nslab, SLAB_GROUPS, SSM_GROUP, SSM_STATE).transpose(0, 1, 3, 2)
        return block_diag(t.reshape(nslab, SLAB_STATES, SSM_GROUP), SSM_STATE, SLAB, SSM_GROUP)

    b_mat = jnp.concatenate([b_block(b_bar.real), b_block(b_bar.imag)], axis=2).astype(BF16)
    c_mat = jnp.concatenate([c_block(c_re), c_block(-c_im)], axis=1).astype(BF16)

    lam1 = lam_bar.reshape(nslab, 1, SLAB_STATES)
    big1, sq, n = jnp.ones_like(lam1), lam1, sub_len
    while n:
        if n & 1:
            big1 = big1 * sq
        sq, n = sq * sq, n >> 1
    big2 = big1 * big1
    big4 = big2 * big2
    rows = jnp.arange(SUBLANES)[None, :, None]
    every = jnp.broadcast_to(lam1, (nslab, SUBLANES, SLAB_STATES))
    planes = [every.real, every.imag]
    for big, shift in ((big1, 1), (big2, 2), (big4, 4)):
        a = jnp.where(rows >= shift, big, 0.0)
        planes += [a.real, a.imag]
    planes.append(jnp.broadcast_to((rows >= 1).astype(F32), (nslab, SUBLANES, SLAB_STATES)))
    mult = jnp.stack(planes, axis=1).astype(F32)
    return b_mat, mult, c_mat


def _ssm(u_perm, b_mat, mult, c_mat, d_skip, layer, *, batch, seq):
    nslab = b_mat.shape[1]
    width = nslab * SLAB
    sub_len = seq // SUBLANES
    assert seq % SSM_CHUNK == 0 and (SSM_CHUNK // SUBLANES) % 2 == 0
    return pl.pallas_call(
        functools.partial(_ssm_kernel, seq=seq),
        grid=(batch, nslab),
        in_specs=[pl.BlockSpec((seq, SLAB), lambda b, s: (b, s)),
                  pl.BlockSpec((None, 1, SLAB, 2 * SLAB_STATES), lambda b, s: (layer, s, 0, 0)),
                  pl.BlockSpec((None, 1, 9, SUBLANES, SLAB_STATES), lambda b, s: (layer, s, 0, 0, 0)),
                  pl.BlockSpec((None, 1, 2 * SLAB_STATES, SLAB), lambda b, s: (layer, s, 0, 0)),
                  pl.BlockSpec((None, 1, SLAB), lambda b, s: (layer, 0, s))],
        out_specs=pl.BlockSpec((seq, SLAB), lambda b, s: (b, s)),
        out_shape=jax.ShapeDtypeStruct((batch * seq, width), BF16),
        scratch_shapes=[pltpu.VMEM((seq, 2 * SLAB_STATES), F32), pltpu.VMEM((seq, 2 * SLAB_STATES), BF16)],
        compiler_params=_params("parallel", "parallel"),
        name="s5_ssm",
    )(u_perm, b_mat, mult, c_mat, d_skip.reshape(-1, 1, width))


def kernel(x, p, g_mix, w_in, w_br_attn, lam_re, lam_im, log_dt, b_re, b_im, c_re, c_im, d_skip, w_glu, w_br_ssm, w_o, g_mlp, w_ff1, w_ff2, g_ple, w_ple_gate, w_ple, g_final):
    bsz, seq, d = x.shape
    depth = w_in.shape[0]
    attn_w = w_br_attn.shape[1]
    ssm_w = w_br_ssm.shape[1]
    heads = attn_w // HEAD_DIM
    t = bsz * seq
    assert w_in.shape[2] == 3 * attn_w + ssm_w + 2 * d

    wb_in = _to_bf16(w_in, g_mix, layer=0)
    wb_ple = _to_bf16(w_ple)
    b_mat, mult, c_mat = jax.vmap(functools.partial(_ssm_operands, sub_len=seq // SUBLANES))(
        lam_re, lam_im, log_dt, b_re, b_im, c_re, c_im)

    def plain(w):
        return jnp.ones(w.shape[:2], F32)

    h = x.reshape(t, d)
    hb, ss = _enter(h)
    for i in range(depth):
        qkv, gates = _proj(hb, ss, wb_in, i, attn_w=attn_w, ssm_w=ssm_w, d_model=d)
        u_perm = _proj_u(hb, ss, wb_in, i, col0=3 * attn_w, width=ssm_w, batch=bsz, seq=seq)
        attn, (wb_ff1, wb_o, wb_ple_gate, wb_br_attn, wb_br_ssm, wb_glu) = _attention(
            qkv,
            [_SideCast(w_ff1, g_mlp, i), _SideCast(w_o, plain(w_o), i), _SideCast(w_ple_gate, g_ple, i),
             _SideCast(w_br_attn, plain(w_br_attn), i), _SideCast(w_br_ssm, plain(w_br_ssm), i),
             _SideCast(w_glu, plain(w_glu), i)],
            batch=bsz, seq=seq, heads=heads)
        y_perm = _ssm(u_perm, b_mat, mult, c_mat, d_skip, i, batch=bsz, seq=seq)
        ssm3 = _glu(y_perm, wb_glu, i, batch=bsz, seq=seq)
        merged = _merge(attn, wb_br_attn, ssm3, wb_br_ssm, gates, i, seq=seq)
        h, hb, ss = _resid_matmul(merged, wb_o, i, h, name="out_proj")

        act, wb_ff2 = _normed_relu2_matmul(hb, ss, wb_ff1, i, name="ff1",
                                           side=_SideCast(w_ff2, plain(w_ff2), i))
        if i + 1 < depth:
            h, hb, ss, wb_in = _resid_matmul(act, wb_ff2, i, h, name="ff2",
                                             side=_SideCast(w_in, g_mix, i + 1))
        else:
            h, hb, ss = _resid_matmul(act, wb_ff2, i, h, name="ff2")

        h, hb, ss = _ple(hb, ss, wb_ple_gate, p[i].reshape(t, -1), wb_ple, i, h)
    return _rmsnorm(h, g_final, x.dtype).reshape(bsz, seq, d)
```
